```python
import math
import jax, jax.numpy as jnp
from jax import lax
import numpy as np

D_MODEL = 1024
BATCH = 8
SEQ = 4096
DEPTH = 2

GRID_W = 64
CTX_LEN = 256
N_MIXERS = 2
N_RWKV = (DEPTH + 1) // 2
N_ATTN = DEPTH // 2
RWKV_HEAD = 64
RWKV_HEADS = D_MODEL // RWKV_HEAD
DECAY_LORA = 64
AAA_LORA = 64
GATE_LORA = 128
GN_EPS = 64e-5
ATTN_HEAD = 64
ATTN_HEADS = D_MODEL // ATTN_HEAD
KV_HEADS = 4
KV_GROUP = ATTN_HEADS // KV_HEADS
Q_BLOCK = 128
ROPE_THETA = 10000.0
D_FF = 4 * D_MODEL
NORM_EPS = 1e-6

kernel_name = "hybrid_rwkv7_gqa_dit_block"


def rms_norm(x, gain, eps=NORM_EPS):
    xf = x.astype(jnp.float32)
    y = xf * lax.rsqrt(jnp.mean(xf * xf, -1, keepdims=True) + eps)
    return (y * gain.astype(jnp.float32)).astype(x.dtype)


def modulate(h, shift, scale):
    return h * (1 + scale) + shift


def sqrelu_mlp(h, w1, w2):
    return jnp.square(jax.nn.relu(h @ w1)) @ w2


def grid_shift(x, rows):
    b, s, d = x.shape
    q = d // 4
    g = x.reshape(b, rows, GRID_W, d)
    left = jnp.pad(g[:, :, :-1, :q], ((0, 0), (0, 0), (1, 0), (0, 0)))
    right = jnp.pad(g[:, :, 1:, q:2 * q], ((0, 0), (0, 0), (0, 1), (0, 0)))
    up = jnp.pad(g[:, :-1, :, 2 * q:3 * q], ((0, 0), (1, 0), (0, 0), (0, 0)))
    down = jnp.pad(g[:, 1:, :, 3 * q:], ((0, 0), (0, 1), (0, 0), (0, 0)))
    return jnp.concatenate([left, right, up, down], -1).reshape(b, s, d)


def seq_shift(x):
    h = x.shape[-1] // 2
    prev = jnp.pad(x[:, :-1, :h], ((0, 0), (1, 0), (0, 0)))
    nxt = jnp.pad(x[:, 1:, h:], ((0, 0), (0, 1), (0, 0)))
    return jnp.concatenate([prev, nxt], -1)


def _wkv_step(state, inp):
    r, w, k, v, kk, bvec = inp
    sa = jnp.einsum('bhvk,bhk->bhv', state, -kk)
    state = state * w[:, :, None, :] + sa[..., None] * bvec[:, :, None, :] + v[..., None] * k[:, :, None, :]
    y = jnp.einsum('bhvk,bhk->bhv', state, r)
    return state, y


def wkv_scan(r, w, k, v, kk, bvec, state0, reverse):
    seqs = tuple(jnp.swapaxes(t, 0, 1) for t in (r, w, k, v, kk, bvec))
    state, y = lax.scan(_wkv_step, state0, seqs, reverse=reverse)
    return state, jnp.swapaxes(y, 0, 1)


def _rwkv_project(h, hs, mu, wr, wk, wv, w0, w1, w2, a0, a1, a2, g1, g2, k_k, k_a):
    b, t, d = h.shape
    hf = h.astype(jnp.float32)
    xx = hs.astype(jnp.float32) - hf
    xr, xw, xk, xv, xa, xg = [hf + xx * mu[j] for j in range(6)]
    split = lambda z: z.reshape(b, t, RWKV_HEADS, RWKV_HEAD)
    r = xr @ wr
    k = xk @ wk
    v = xv @ wv
    g = jax.nn.sigmoid(xg @ g1) @ g2
    kk = split(k * k_k)
    kk = kk * lax.rsqrt(jnp.sum(kk * kk, -1, keepdims=True) + 1e-12)
    dirs = []
    for dr in range(2):
        w_log = -jax.nn.softplus(-(w0[dr] + jnp.tanh(xw @ w1[dr]) @ w2[dr])) - 0.5
        decay = jnp.exp(-jnp.exp(w_log))
        a = jax.nn.sigmoid(a0[dr] + (xa @ a1[dr]) @ a2[dr])
        k_dir = k * (1 + (a - 1) * k_a)
        dirs.append((split(decay), split(k_dir), split(a)))
    return split(r), split(v), g, kk, dirs


def _rwkv_scans(p, init_f, init_b):
    r, v, g, kk, dirs = p
    (dec_f, k_f, a_f), (dec_b, k_b, a_b) = dirs
    s_f, y_f = wkv_scan(r, dec_f, k_f, v, kk, kk * a_f, init_f, False)
    s_b, y_b = wkv_scan(r, dec_b, k_b, v, kk, kk * a_b, init_b, True)
    return s_f, s_b, y_f + y_b


def _rwkv_readout(y, p, r_k, ln_w, ln_b, wo, out_dtype):
    r, v, g, kk, dirs = p
    b, t = y.shape[:2]
    mean = jnp.mean(y, -1, keepdims=True)
    var = jnp.mean(jnp.square(y - mean), -1, keepdims=True)
    yn = ((y - mean) * lax.rsqrt(var + GN_EPS)).reshape(b, t, D_MODEL) * ln_w + ln_b
    bonus = (jnp.sum(r * dirs[0][1] * r_k, -1, keepdims=True)
             + jnp.sum(r * dirs[1][1] * r_k, -1, keepdims=True)) * v
    out = (yn + bonus.reshape(b, t, D_MODEL)) * g
    return (out @ wo).astype(out_dtype)


def rwkv_mix(h_lat, h_ctx, rows, need_ctx, mu, wr, wk, wv, wo, w0, w1, w2, a0, a1, a2, g1, g2, k_k, k_a, r_k, ln_w, ln_b):
    pw = (mu, wr, wk, wv, w0, w1, w2, a0, a1, a2, g1, g2, k_k, k_a)
    zero = jnp.zeros((h_lat.shape[0], RWKV_HEADS, RWKV_HEAD, RWKV_HEAD), jnp.float32)
    p_ctx = _rwkv_project(h_ctx, seq_shift(h_ctx), *pw)
    s_f, s_b, y_ctx = _rwkv_scans(p_ctx, zero, zero)
    p_lat = _rwkv_project(h_lat, grid_shift(h_lat, rows), *pw)
    _, _, y_lat = _rwkv_scans(p_lat, s_f, s_b)
    o_lat = _rwkv_readout(y_lat, p_lat, r_k, ln_w, ln_b, wo, h_lat.dtype)
    o_ctx = _rwkv_readout(y_ctx, p_ctx, r_k, ln_w, ln_b, wo, h_ctx.dtype) if need_ctx else None
    return o_lat, o_ctx


def rope_2d_tables(n_tokens):
    t = jnp.arange(n_tokens)
    row = (t // GRID_W).astype(jnp.float32)
    col = (t % GRID_W).astype(jnp.float32)
    half = ATTN_HEAD // 2
    freqs = ROPE_THETA ** (-jnp.arange(0, half, 2, dtype=jnp.float32) / half)
    ang = jnp.stack([row[:, None] * freqs, col[:, None] * freqs], 0)
    return jnp.cos(ang), jnp.sin(ang)


def apply_rope_2d(x, cos, sin):
    half = ATTN_HEAD // 2
    quarter = half // 2
    outs = []
    for ax in range(2):
        xa = x[..., ax * half:(ax + 1) * half]
        x1, x2 = xa[..., :quarter], xa[..., quarter:]
        c = cos[ax][None, :, None, :]
        s = sin[ax][None, :, None, :]
        outs += [x1 * c - x2 * s, x2 * c + x1 * s]
    return jnp.concatenate(outs, -1)


def attn_mix(h_lat, h_ctx, cos, sin, need_ctx, wqkv, q_norm, k_norm, wo):
    nq = ATTN_HEADS * ATTN_HEAD
    nk = KV_HEADS * ATTN_HEAD
    scale = ATTN_HEAD ** -0.5

    def project(h):
        b, t, _ = h.shape
        qkv = h @ wqkv
        q = qkv[..., :nq].reshape(b, t, ATTN_HEADS, ATTN_HEAD)
        k = qkv[..., nq:nq + nk].reshape(b, t, KV_HEADS, ATTN_HEAD)
        v = qkv[..., nq + nk:].reshape(b, t, KV_HEADS, ATTN_HEAD)
        q = rms_norm(q, q_norm).astype(jnp.float32)
        k = rms_norm(k, k_norm).astype(jnp.float32)
        return q, k, v.astype(jnp.float32)

    ql, kl, vl = project(h_lat)
    ql = apply_rope_2d(ql, cos, sin)
    kl = apply_rope_2d(kl, cos, sin)
    qc, kc, vc = project(h_ctx)
    k_all = jnp.concatenate([kl, kc], 1)
    v_all = jnp.concatenate([vl, vc], 1)
    b, s = ql.shape[:2]
    nb = s // Q_BLOCK
    qb = ql.reshape(b, nb, Q_BLOCK, KV_HEADS, KV_GROUP, ATTN_HEAD).transpose(1, 0, 2, 3, 4, 5)

    def block(q):
        sc = jnp.einsum('bqkgd,bskd->bkgqs', q, k_all) * scale
        p = jax.nn.softmax(sc, -1)
        return jnp.einsum('bkgqs,bskd->bqkgd', p, v_all)

    o = lax.map(block, qb)
    o_lat = (o.transpose(1, 0, 2, 3, 4, 5).reshape(b, s, nq) @ wo).astype(h_lat.dtype)
    o_ctx = None
    if need_ctx:
        c_len = qc.shape[1]
        qg = qc.reshape(b, c_len, KV_HEADS, KV_GROUP, ATTN_HEAD)
        p = jax.nn.softmax(jnp.einsum('bqkgd,bskd->bkgqs', qg, kc) * scale, -1)
        oc = jnp.einsum('bkgqs,bskd->bqkgd', p, vc).reshape(b, c_len, nq)
        o_ctx = (oc @ wo).astype(h_ctx.dtype)
    return o_lat, o_ctx


def setup_inputs(seed: int = 0) -> dict:
    key = jax.random.key(seed)
    ks = iter(jax.random.split(key, 48))
    D = D_MODEL
    nrm = lambda shape, std: jax.random.normal(next(ks), shape, jnp.float32) * std
    uni = lambda shape, lo, hi: jax.random.uniform(next(ks), shape, jnp.float32, lo, hi)
    qkv_w = (ATTN_HEADS + 2 * KV_HEADS) * ATTN_HEAD
    return {
        "x": nrm((BATCH, SEQ, D), 1.0),
        "c": nrm((BATCH, D), 1.0),
        "ctx": nrm((BATCH, CTX_LEN, D), 1.0),
        "c_ctx": nrm((D,), 1.0),
        "w_mod": nrm((DEPTH, D, 6 * D), 0.5 * D ** -0.5),
        "b_mod": nrm((DEPTH, 6 * D), 0.01),
        "norm_mix": 1.0 + nrm((DEPTH, D), 0.05),
        "norm_mlp": 1.0 + nrm((DEPTH, D), 0.05),
        "mlp_w1": nrm((DEPTH, D, D_FF), D ** -0.5),
        "mlp_w2": nrm((DEPTH, D_FF, D), D_FF ** -0.5),
        "rwkv_mu": uni((N_RWKV, 6, D), 0.0, 1.0),
        "rwkv_wr": nrm((N_RWKV, D, D), D ** -0.5),
        "rwkv_wk": nrm((N_RWKV, D, D), D ** -0.5),
        "rwkv_wv": nrm((N_RWKV, D, D), D ** -0.5),
        "rwkv_wo": nrm((N_RWKV, D, D), D ** -0.5),
        "rwkv_w0": uni((N_RWKV, 2, D), -4.0, 1.0),
        "rwkv_w1": nrm((N_RWKV, 2, D, DECAY_LORA), D ** -0.5),
        "rwkv_w2": nrm((N_RWKV, 2, DECAY_LORA, D), 0.5 * DECAY_LORA ** -0.5),
        "rwkv_a0": nrm((N_RWKV, 2, D), 0.1),
        "rwkv_a1": nrm((N_RWKV, 2, D, AAA_LORA), D ** -0.5),
        "rwkv_a2": nrm((N_RWKV, 2, AAA_LORA, D), 0.5 * AAA_LORA ** -0.5),
        "rwkv_g1": nrm((N_RWKV, D, GATE_LORA), D ** -0.5),
        "rwkv_g2": nrm((N_RWKV, GATE_LORA, D), GATE_LORA ** -0.5),
        "rwkv_k_k": 0.85 + nrm((N_RWKV, D), 0.05),
        "rwkv_k_a": 1.0 + nrm((N_RWKV, D), 0.05),
        "rwkv_r_k": nrm((N_RWKV, RWKV_HEADS, RWKV_HEAD), 0.1),
        "rwkv_ln_w": 1.0 + nrm((N_RWKV, D), 0.05),
        "rwkv_ln_b": nrm((N_RWKV, D), 0.01),
        "attn_wqkv": nrm((N_ATTN, D, qkv_w), D ** -0.5),
        "attn_q_norm": 1.0 + nrm((N_ATTN, ATTN_HEAD), 0.05),
        "attn_k_norm": 1.0 + nrm((N_ATTN, ATTN_HEAD), 0.05),
        "attn_wo": nrm((N_ATTN, ATTN_HEADS * ATTN_HEAD, D), (ATTN_HEADS * ATTN_HEAD) ** -0.5),
        "final_norm": 1.0 + nrm((D,), 0.05),
    }


def reference(x, c, ctx, c_ctx, w_mod, b_mod, norm_mix, norm_mlp, mlp_w1, mlp_w2,
              rwkv_mu, rwkv_wr, rwkv_wk, rwkv_wv, rwkv_wo, rwkv_w0, rwkv_w1, rwkv_w2,
              rwkv_a0, rwkv_a1, rwkv_a2, rwkv_g1, rwkv_g2, rwkv_k_k, rwkv_k_a, rwkv_r_k,
              rwkv_ln_w, rwkv_ln_b, attn_wqkv, attn_q_norm, attn_k_norm, attn_wo, final_norm):
    s = x.shape[1]
    rows = s // GRID_W
    cos, sin = rope_2d_tables(s)
    sc = jax.nn.silu(c)
    scc = jax.nn.silu(c_ctx)
    xl, xc = x, ctx
    for i in range(DEPTH):
        j = i // N_MIXERS
        need_ctx = i < DEPTH - 1
        m_l = jnp.split(sc @ w_mod[i] + b_mod[i], 6, -1)
        m_c = jnp.split(scc @ w_mod[i] + b_mod[i], 6, -1)
        h_l = modulate(rms_norm(xl, norm_mix[i]), m_l[0][:, None], m_l[1][:, None])
        h_c = modulate(rms_norm(xc, norm_mix[i]), m_c[0], m_c[1])
        if i % N_MIXERS == 0:
            o_l, o_c = rwkv_mix(h_l, h_c, rows, need_ctx, rwkv_mu[j], rwkv_wr[j], rwkv_wk[j], rwkv_wv[j], rwkv_wo[j],
                                rwkv_w0[j], rwkv_w1[j], rwkv_w2[j], rwkv_a0[j], rwkv_a1[j], rwkv_a2[j],
                                rwkv_g1[j], rwkv_g2[j], rwkv_k_k[j], rwkv_k_a[j], rwkv_r_k[j],
                                rwkv_ln_w[j], rwkv_ln_b[j])
        else:
            o_l, o_c = attn_mix(h_l, h_c, cos, sin, need_ctx, attn_wqkv[j], attn_q_norm[j],
                                attn_k_norm[j], attn_wo[j])
        xl = xl + m_l[2][:, None] * o_l
        h_l = modulate(rms_norm(xl, norm_mlp[i]), m_l[3][:, None], m_l[4][:, None])
        xl = xl + m_l[5][:, None] * sqrelu_mlp(h_l, mlp_w1[i], mlp_w2[i])
        if need_ctx:
            xc = xc + m_c[2] * o_c
            h_c = modulate(rms_norm(xc, norm_mlp[i]), m_c[3], m_c[4])
            xc = xc + m_c[5] * sqrelu_mlp(h_c, mlp_w1[i], mlp_w2[i])
    return rms_norm(xl, final_norm).astype(x.dtype)
```

```python
import functools
import math

import jax
import jax.numpy as jnp
from jax import lax
from jax.experimental import pallas as pl
from jax.experimental.pallas import tpu as pltpu

F32 = jnp.float32
BF16 = jnp.bfloat16

HEAD = 64
GRID_W = 64
KV_HEADS = 4
NORM_EPS = 1e-6
GN_EPS = 64e-5
ROPE_THETA = 10000.0
CHUNK = 64
TM = 256
SEG_W = 128
WKV_HEADS_PER_STEP = 4
VMEM_LIMIT = 56 * 1024 * 1024
EXP_NEG_HALF = math.exp(-0.5)


def _bdot(a, b):
    return jnp.dot(a.astype(BF16), b.astype(BF16), preferred_element_type=F32)


def _split_hi_lo(x):
    hi = x.astype(BF16)
    lo = (x - hi.astype(F32)).astype(BF16)
    return hi, lo


def _seg_sum(x, seg, segt):
    hi, lo = _split_hi_lo(x)
    s = jnp.dot(hi, seg, preferred_element_type=F32) + jnp.dot(lo, seg, preferred_element_type=F32)
    s_hi, s_lo = _split_hi_lo(s)
    return jnp.dot(s_hi, segt, preferred_element_type=F32) + jnp.dot(s_lo, segt, preferred_element_type=F32)


def _norm_mod(x, gain, shift, scale):
    ms = jnp.mean(x * x, axis=-1, keepdims=True)
    y = x * lax.rsqrt(ms + NORM_EPS) * gain
    return y * (1.0 + scale) + shift


def _cparams(sem):
    return pltpu.CompilerParams(dimension_semantics=sem, vmem_limit_bytes=VMEM_LIMIT)


def _const_spec(shape):
    nd = len(shape)
    return pl.BlockSpec(shape, lambda *_: (0,) * nd)


def _mod_kernel(c_ref, w_ref, b_ref, o_ref):
    c = c_ref[...]
    s = c * jax.nn.sigmoid(c)
    s_hi, s_lo = _split_hi_lo(s)
    w = w_ref[0]
    w_hi, w_lo = _split_hi_lo(w)
    acc = jnp.dot(s_hi, w_hi, preferred_element_type=F32)
    acc += jnp.dot(s_hi, w_lo, preferred_element_type=F32)
    acc += jnp.dot(s_lo, w_hi, preferred_element_type=F32)
    o_ref[0] = acc + b_ref[0]


def _modulation(cvec, w_mod, b_mod):
    depth, d, n = w_mod.shape
    rows = cvec.shape[0]
    tn = 512
    return pl.pallas_call(
        _mod_kernel,
        grid=(depth, n // tn),
        in_specs=[
            pl.BlockSpec((rows, d), lambda l, j: (0, 0)),
            pl.BlockSpec((1, d, tn), lambda l, j: (l, 0, j)),
            pl.BlockSpec((1, 1, tn), lambda l, j: (l, 0, j)),
        ],
        out_specs=pl.BlockSpec((1, rows, tn), lambda l, j: (l, 0, j)),
        out_shape=jax.ShapeDtypeStruct((depth, rows, n), F32),
        compiler_params=_cparams(("parallel", "parallel")),
        name="adaln_modulation",
    )(cvec, w_mod, b_mod.reshape(depth, 1, n))


def _proj0_kernel(nt, xc_ref, xp_ref, xn_ref, mod_ref, gain_ref, mu_ref, wr_ref, wk_ref, wv_ref, g1_ref, g2_ref,
                  w1_ref, w2_ref, w0_ref, a1_ref, a2_ref, a0_ref, kk_ref, ka_ref, rk_ref, seg_ref, segt_ref,
                  trif_ref, trib_ref,
                  v_out, g_out, bonus_out, af_out, rf_out, bf_out, kf_out, ab_out, rb_out, bb_out, kb_out,
                  dlf_out, dlb_out, hs_ref):
    i = pl.program_id(1)
    d = xc_ref.shape[-1]
    q = d // 4
    gain = gain_ref[...]
    shift = mod_ref[0, 0, 0:1, :]
    scale = mod_ref[0, 0, 1:2, :]
    h = _norm_mod(xc_ref[0], gain, shift, scale)
    t_idx = lax.broadcasted_iota(jnp.int32, (TM, 1), 0)

    @pl.when(i == 0)
    def _():
        prev = jnp.where(t_idx == 0, 0.0, pltpu.roll(h[:, :2 * q], 1, 0))
        nxt = jnp.where(t_idx == TM - 1, 0.0, pltpu.roll(h[:, 2 * q:], TM - 1, 0))
        hs_ref[:, :2 * q] = prev
        hs_ref[:, 2 * q:] = nxt

    @pl.when(i > 0)
    def _():
        col = t_idx % GRID_W
        left = jnp.where(col == 0, 0.0, pltpu.roll(h[:, :q], 1, 0))
        right = jnp.where(col == GRID_W - 1, 0.0, pltpu.roll(h[:, q:2 * q], TM - 1, 0))
        hp = _norm_mod(xp_ref[0], gain, shift, scale)[:, 2 * q:3 * q]
        hn = _norm_mod(xn_ref[0], gain, shift, scale)[:, 3 * q:]
        hp = jnp.where(i > 1, hp, 0.0)
        hn = jnp.where(i < nt - 1, hn, 0.0)
        hs_ref[:, :q] = left
        hs_ref[:, q:2 * q] = right
        hs_ref[:, 2 * q:3 * q] = jnp.concatenate([hp, h[:TM - GRID_W, 2 * q:3 * q]], axis=0)
        hs_ref[:, 3 * q:] = jnp.concatenate([h[GRID_W:, 3 * q:], hn], axis=0)

    xx = hs_ref[...] - h
    mix = lambda j: (h + xx * mu_ref[j:j + 1, :]).astype(BF16)
    seg = seg_ref[...]
    segt = segt_ref[...]

    r = jnp.dot(mix(0), wr_ref[...], preferred_element_type=F32)
    k = jnp.dot(mix(2), wk_ref[...], preferred_element_type=F32)
    v = jnp.dot(mix(3), wv_ref[...], preferred_element_type=F32)
    g = _bdot(jax.nn.sigmoid(jnp.dot(mix(5), g1_ref[...], preferred_element_type=F32)), g2_ref[...])
    wl = _bdot(jnp.tanh(jnp.dot(mix(1), w1_ref[...], preferred_element_type=F32)), w2_ref[...]) + w0_ref[...]
    al = _bdot(jnp.dot(mix(4), a1_ref[...], preferred_element_type=F32), a2_ref[...]) + a0_ref[...]
    v_out[0] = v.astype(BF16)
    g_out[0] = g.astype(BF16)

    kk = k * kk_ref[...]
    kk = kk * lax.rsqrt(_seg_sum(kk * kk, seg, segt) + 1e-12)

    outs = ((af_out, rf_out, bf_out, kf_out, dlf_out, trif_ref), (ab_out, rb_out, bb_out, kb_out, dlb_out, trib_ref))
    ksum = None
    for dr, (a_o, r_o, b_o, k_o, dl_o, tri_ref) in enumerate(outs):
        e = EXP_NEG_HALF * jax.nn.sigmoid(wl[:, dr * d:(dr + 1) * d])
        a_lr = jax.nn.sigmoid(al[:, dr * d:(dr + 1) * d])
        k_dir = k * (1.0 + (a_lr - 1.0) * ka_ref[...])
        ksum = k_dir if ksum is None else ksum + k_dir
        e_hi, e_lo = _split_hi_lo(e)
        tri = tri_ref[...]
        c = jnp.dot(tri, e_hi, preferred_element_type=F32) + jnp.dot(tri, e_lo, preferred_element_type=F32)
        grow = jnp.exp(c)
        shrink = jnp.exp(-c)
        a_o[0] = (-kk * jnp.exp(e - c)).astype(BF16)
        r_o[0] = (r * shrink).astype(BF16)
        b_o[0] = (kk * a_lr * grow).astype(BF16)
        k_o[0] = (k_dir * grow).astype(BF16)
        last = CHUNK - 1 if dr == 0 else 0
        dl_o[0, 0] = jnp.concatenate(
            [shrink[n * CHUNK + last:n * CHUNK + last + 1, :] for n in range(TM // CHUNK)], axis=0)

    bonus_out[0] = (_seg_sum(r * ksum * rk_ref[...], seg, segt) * v).astype(BF16)


def _proj0(xall, mods, gain, p, consts, ctx_tiles):
    b, t, d = xall.shape
    nt = t // TM
    hb = TM // GRID_W
    nhb = t // GRID_W
    tok = lambda bb, i: (bb, i, 0)
    big = jax.ShapeDtypeStruct((b, t, d), BF16)
    dl = jax.ShapeDtypeStruct((b, nt, TM // CHUNK, d), F32)
    tile_spec = pl.BlockSpec((1, TM, d), tok)
    dl_spec = pl.BlockSpec((1, 1, TM // CHUNK, d), lambda bb, i: (bb, i, 0, 0))
    weights = [p["mu"], p["wr"], p["wk"], p["wv"], p["g1"], p["g2"], p["w1"], p["w2"], p["w0"],
               p["a1"], p["a2"], p["a0"], p["k_k"], p["k_a"], p["r_k"],
               consts["seg"], consts["segt"], consts["tri_f"], consts["tri_b"]]
    return pl.pallas_call(
        functools.partial(_proj0_kernel, nt),
        grid=(b, nt),
        in_specs=[
            tile_spec,
            pl.BlockSpec((1, GRID_W, d), lambda bb, i: (bb, jnp.maximum(i * hb - 1, 0), 0)),
            pl.BlockSpec((1, GRID_W, d), lambda bb, i: (bb, jnp.minimum((i + 1) * hb, nhb - 1), 0)),
            pl.BlockSpec((1, 1, 6, d), lambda bb, i: (bb, jnp.minimum(i, 1), 0, 0)),
            _const_spec(gain.shape),
        ] + [_const_spec(w.shape) for w in weights],
        out_specs=[tile_spec] * 11 + [dl_spec] * 2,
        out_shape=[big] * 11 + [dl] * 2,
        scratch_shapes=[pltpu.VMEM((TM, d), F32)],
        compiler_params=_cparams(("parallel", "parallel")),
        name="rwkv_project",
    )(xall, xall, xall, mods, gain, *weights)


def _wkv_kernel(reverse, ncc, a_ref, r_ref, b_ref, k_ref, v_ref, dl_ref, y_ref):
    nc = dl_ref.shape[1]
    hps = a_ref.shape[-1] // HEAD
    row = lax.broadcasted_iota(jnp.int32, (CHUNK, CHUNK), 0)
    col = lax.broadcasted_iota(jnp.int32, (CHUNK, CHUNK), 1)
    strict = (col > row) if reverse else (col < row)
    incl = (col >= row) if reverse else (col <= row)
    eye = (row == col).astype(F32)
    nt_dims = (((1,), (1,)), ((), ()))
    tn_dims = (((0,), (0,)), ((), ()))

    def step(ci, states):
        c = jnp.where(ci < ncc, ncc - 1 - ci, nc - 1 - ci + ncc) if reverse else ci
        r0 = pl.multiple_of(c * CHUNK, CHUNK)
        a_all = a_ref[0, pl.ds(r0, CHUNK), :]
        r_all = r_ref[0, pl.ds(r0, CHUNK), :]
        b_all = b_ref[0, pl.ds(r0, CHUNK), :]
        k_all = k_ref[0, pl.ds(r0, CHUNK), :]
        v_all = v_ref[0, pl.ds(r0, CHUNK), :]
        dl_all = dl_ref[0, pl.ds(c, 1), :]
        new_states = []
        ys = []
        for hh in range(hps):
            sl = slice(hh * HEAD, (hh + 1) * HEAD)
            a_h, r_h, b_h, k_h, v_h = a_all[:, sl], r_all[:, sl], b_all[:, sl], k_all[:, sl], v_all[:, sl]
            s0 = states[hh]
            ar = jnp.concatenate([a_h, r_h], axis=0)
            bk = jnp.concatenate([b_h, k_h], axis=0)
            sc = lax.dot_general(ar, bk, nt_dims, preferred_element_type=F32)
            a_ab = jnp.where(strict, sc[:CHUNK, :CHUNK], 0.0)
            a_ak = jnp.where(strict, sc[:CHUNK, CHUNK:], 0.0)
            a_rb = jnp.where(incl, sc[CHUNK:, :CHUNK], 0.0)
            a_rk = jnp.where(incl, sc[CHUNK:, CHUNK:], 0.0)
            pw = a_ab.astype(BF16)
            inv = eye + a_ab
            pw = jnp.dot(pw, pw, preferred_element_type=F32).astype(BF16)
            for _ in range(4):
                res = jnp.dot(pw, jnp.concatenate([pw, inv.astype(BF16)], axis=1), preferred_element_type=F32)
                pw = res[:, :CHUNK].astype(BF16)
                inv = inv + res[:, CHUNK:]
            inv = inv + jnp.dot(pw, inv.astype(BF16), preferred_element_type=F32)
            akv = jnp.dot(a_ak.astype(BF16), v_h, preferred_element_type=F32)
            wu = jnp.dot(inv.astype(BF16), jnp.concatenate([a_h, akv.astype(BF16)], axis=1),
                         preferred_element_type=F32)
            w_r = jnp.concatenate([wu[:, :HEAD].astype(BF16), r_h], axis=0)
            ws = lax.dot_general(w_r, s0.astype(BF16), nt_dims, preferred_element_type=F32)
            u = ws[:CHUNK] + wu[:, HEAD:]
            uv = jnp.concatenate([u.astype(BF16), v_h], axis=0)
            y = ws[CHUNK:] + jnp.dot(jnp.concatenate([a_rb, a_rk], axis=1).astype(BF16), uv,
                                     preferred_element_type=F32)
            upd = lax.dot_general(uv, bk, tn_dims, preferred_element_type=F32)
            new_states.append((s0 + upd) * dl_all[:, sl])
            ys.append(y)
        y_ref[0, pl.ds(r0, CHUNK), :] = jnp.concatenate(ys, axis=1)
        return tuple(new_states)

    init = tuple(jnp.zeros((HEAD, HEAD), F32) for _ in range(hps))
    lax.fori_loop(0, nc, step, init)


def _wkv(reverse, ncc, a, r, bm, k, v, dl):
    b, t, d = a.shape
    lanes = WKV_HEADS_PER_STEP * HEAD
    nc = t // CHUNK
    seq = pl.BlockSpec((1, t, lanes), lambda bb, hg: (bb, 0, hg))
    return pl.pallas_call(
        functools.partial(_wkv_kernel, reverse, ncc),
        grid=(b, d // lanes),
        in_specs=[seq] * 5 + [pl.BlockSpec((1, nc, lanes), lambda bb, hg: (bb, 0, hg))],
        out_specs=seq,
        out_shape=jax.ShapeDtypeStruct((b, t, d), F32),
        compiler_params=_cparams(("parallel", "parallel")),
        name="wkv_bwd" if reverse else "wkv_fwd",
    )(a, r, bm, k, v, dl)


def _mlp(h, w1_ref, w2_ref):
    hb = h.astype(BF16)
    dff = w1_ref.shape[1]
    fc = 1024
    acc = None
    for j in range(dff // fc):
        a = jnp.dot(hb, w1_ref[:, j * fc:(j + 1) * fc], preferred_element_type=F32)
        a = jnp.square(jnp.maximum(a, 0.0)).astype(BF16)
        part = jnp.dot(a, w2_ref[j * fc:(j + 1) * fc, :], preferred_element_type=F32)
        acc = part if acc is None else acc + part
    return acc


def _readout0_kernel(x_ref, yf_ref, yb_ref, bonus_ref, g_ref, mod_ref, lnw_ref, lnb_ref, wo_ref, gain_ref,
                     w1_ref, w2_ref, seg_ref, segt_ref, o_ref):
    seg = seg_ref[...]
    segt = segt_ref[...]
    y = yf_ref[0] + yb_ref[0]
    mean = _seg_sum(y, seg, segt) * (1.0 / HEAD)
    dev = y - mean
    var = _seg_sum(dev * dev, seg, segt) * (1.0 / HEAD)
    yn = dev * lax.rsqrt(var + GN_EPS) * lnw_ref[...] + lnb_ref[...]
    mixed = (yn + bonus_ref[0].astype(F32)) * g_ref[0].astype(F32)
    o = _bdot(mixed, wo_ref[...])
    x1 = x_ref[0] + mod_ref[0, 0, 2:3, :] * o
    h2 = _norm_mod(x1, gain_ref[...], mod_ref[0, 0, 3:4, :], mod_ref[0, 0, 4:5, :])
    o_ref[0] = x1 + mod_ref[0, 0, 5:6, :] * _mlp(h2, w1_ref, w2_ref)


def _readout0(xall, yf, yb, bonus, g, mods, p, gain_mlp, w1, w2, consts):
    b, t, d = xall.shape
    nt = t // TM
    tile = pl.BlockSpec((1, TM, d), lambda bb, i: (bb, i, 0))
    weights = [p["ln_w"], p["ln_b"], p["wo"], gain_mlp, w1, w2, consts["seg"], consts["segt"]]
    return pl.pallas_call(
        _readout0_kernel,
        grid=(b, nt),
        in_specs=[tile] * 5 + [pl.BlockSpec((1, 1, 6, d), lambda bb, i: (bb, jnp.minimum(i, 1), 0, 0))]
        + [_const_spec(w.shape) for w in weights],
        out_specs=tile,
        out_shape=jax.ShapeDtypeStruct((b, t, d), F32),
        compiler_params=_cparams(("parallel", "parallel")),
        name="rwkv_readout_mlp",
    )(xall, yf, yb, bonus, g, mods, *weights)


def _rope(x, cos, sin_signed):
    lane = lax.broadcasted_iota(jnp.int32, (1, 128), 1)
    low = (lane & 16) == 0
    cols = []
    for j in range(x.shape[1] // 128):
        xj = x[:, j * 128:(j + 1) * 128]
        partner = jnp.where(low, pltpu.roll(xj, 128 - 16, 1), pltpu.roll(xj, 16, 1))
        cols.append(xj * cos + partner * sin_signed)
    return jnp.concatenate(cols, axis=1)


def _qkv1_kernel(x_ref, mod_ref, gain_ref, w_ref, qn_ref, kn_ref, cos_ref, sin_ref, seg_ref, segt_ref,
                 segk_ref, segkt_ref, q_out, k_out, v_out):
    d = x_ref.shape[-1]
    nk = KV_HEADS * HEAD
    h = _norm_mod(x_ref[0], gain_ref[...], mod_ref[0, 0, 0:1, :], mod_ref[0, 0, 1:2, :])
    qkv = _bdot(h, w_ref[...])
    cos = cos_ref[...]
    sin = sin_ref[...]
    q = qkv[:, :d]
    q = q * lax.rsqrt(_seg_sum(q * q, seg_ref[...], segt_ref[...]) * (1.0 / HEAD) + NORM_EPS) * qn_ref[...]
    q_out[0] = (_rope(q, cos, sin) * (HEAD ** -0.5)).astype(BF16)
    k = qkv[:, d:d + nk]
    k = k * lax.rsqrt(_seg_sum(k * k, segk_ref[...], segkt_ref[...]) * (1.0 / HEAD) + NORM_EPS) * kn_ref[...]
    k = _rope(k, cos, sin).astype(BF16)
    v = qkv[:, d + nk:].astype(BF16)
    for gi in range(KV_HEADS):
        k_out[0, gi] = k[:, gi * HEAD:(gi + 1) * HEAD]
        v_out[0, gi] = v[:, gi * HEAD:(gi + 1) * HEAD]


def _qkv1(x2, mods, gain, wqkv, qn, kn, cos, sin, consts):
    b, t, d = x2.shape
    nt = t // TM
    tile = pl.BlockSpec((1, TM, d), lambda bb, i: (bb, i, 0))
    kv_shape = jax.ShapeDtypeStruct((b, KV_HEADS, t, HEAD), BF16)
    kv_spec = pl.BlockSpec((1, KV_HEADS, TM, HEAD), lambda bb, i: (bb, 0, i, 0))
    tab = pl.BlockSpec((TM, 128), lambda bb, i: (i, 0))
    weights = [gain, wqkv, qn, kn]
    segs = [consts["seg"], consts["segt"], consts["segk"], consts["segkt"]]
    return pl.pallas_call(
        _qkv1_kernel,
        grid=(b, nt),
        in_specs=[tile, pl.BlockSpec((1, 1, 6, d), lambda bb, i: (bb, jnp.minimum(i, 1), 0, 0))]
        + [_const_spec(w.shape) for w in weights] + [tab, tab] + [_const_spec(w.shape) for w in segs],
        out_specs=[tile, kv_spec, kv_spec],
        out_shape=[jax.ShapeDtypeStruct((b, t, d), BF16), kv_shape, kv_shape],
        compiler_params=_cparams(("parallel", "parallel")),
        name="attn_qkv",
    )(x2, mods, *weights, cos, sin, *segs)


def _attn_kernel(q_ref, k_ref, v_ref, o_ref):
    k = k_ref[0, 0]
    v = v_ref[0, 0]
    nt_dims = (((1,), (1,)), ((), ()))
    outs = []
    for j in range(q_ref.shape[-1] // HEAD):
        qj = q_ref[0, :, j * HEAD:(j + 1) * HEAD]
        s = lax.dot_general(qj, k, nt_dims, preferred_element_type=F32)
        m = jnp.max(s, axis=-1, keepdims=True)
        p = jnp.exp(s - m)
        l = jnp.sum(p, axis=-1, keepdims=True)
        o = jnp.dot(p.astype(BF16), v, preferred_element_type=F32)
        outs.append(o / l)
    o_ref[0] = jnp.concatenate(outs, axis=1).astype(BF16)


def _attention(q, k, v, ctx_tiles):
    b, t, d = q.shape
    s = t - ctx_tiles * TM
    gw = d // KV_HEADS
    kv_spec = pl.BlockSpec((1, 1, t, HEAD), lambda bb, gi, i: (bb, gi, 0, 0))
    return pl.pallas_call(
        _attn_kernel,
        grid=(b, KV_HEADS, s // TM),
        in_specs=[pl.BlockSpec((1, TM, gw), lambda bb, gi, i: (bb, i + ctx_tiles, gi)), kv_spec, kv_spec],
        out_specs=pl.BlockSpec((1, TM, gw), lambda bb, gi, i: (bb, i, gi)),
        out_shape=jax.ShapeDtypeStruct((b, s, d), BF16),
        compiler_params=_cparams(("parallel", "parallel", "parallel")),
        name="gqa_attention",
    )(q, k, v)


def _out1_kernel(x_ref, a_ref, mod_ref, wo_ref, gain_ref, w1_ref, w2_ref, fin_ref, o_ref):
    o = jnp.dot(a_ref[0], wo_ref[...], preferred_element_type=F32)
    x1 = x_ref[0] + mod_ref[0, 0, 2:3, :] * o
    h2 = _norm_mod(x1, gain_ref[...], mod_ref[0, 0, 3:4, :], mod_ref[0, 0, 4:5, :])
    x2 = x1 + mod_ref[0, 0, 5:6, :] * _mlp(h2, w1_ref, w2_ref)
    ms = jnp.mean(x2 * x2, axis=-1, keepdims=True)
    o_ref[0] = x2 * lax.rsqrt(ms + NORM_EPS) * fin_ref[...]


def _out1(x2, att, mods, wo, gain_mlp, w1, w2, fin, ctx_tiles):
    b, s, d = att.shape
    weights = [wo, gain_mlp, w1, w2, fin]
    return pl.pallas_call(
        _out1_kernel,
        grid=(b, s // TM),
        in_specs=[pl.BlockSpec((1, TM, d), lambda bb, i: (bb, i + ctx_tiles, 0)),
                  pl.BlockSpec((1, TM, d), lambda bb, i: (bb, i, 0)),
                  pl.BlockSpec((1, 1, 6, d), lambda bb, i: (bb, 1, 0, 0))]
        + [_const_spec(w.shape) for w in weights],
        out_specs=pl.BlockSpec((1, TM, d), lambda bb, i: (bb, i, 0)),
        out_shape=jax.ShapeDtypeStruct((b, s, d), F32),
        compiler_params=_cparams(("parallel", "parallel")),
        name="attn_out_mlp_final",
    )(x2, att, mods, *weights)


def _block_diag2(a, b):
    za = jnp.zeros((a.shape[0], b.shape[1]), a.dtype)
    zb = jnp.zeros((b.shape[0], a.shape[1]), b.dtype)
    return jnp.concatenate([jnp.concatenate([a, za], 1), jnp.concatenate([zb, b], 1)], 0)


def _constants(d):
    lane = jnp.arange(d)[:, None] // HEAD
    seg = (lane == jnp.arange(SEG_W)[None, :]).astype(BF16)
    nk = KV_HEADS * HEAD
    segk = seg[:nk]
    t = jnp.arange(TM)
    same = (t[:, None] // CHUNK) == (t[None, :] // CHUNK)
    tri_f = (same & (t[None, :] <= t[:, None])).astype(BF16)
    tri_b = (same & (t[None, :] >= t[:, None])).astype(BF16)
    return dict(seg=seg, segt=seg.T, segk=segk, segkt=segk.T, tri_f=tri_f, tri_b=tri_b)


def _rope_tables(s, c_len):
    tok = jnp.arange(s)
    row = (tok // GRID_W).astype(F32)
    col = (tok % GRID_W).astype(F32)
    half = HEAD // 2
    freqs = ROPE_THETA ** (-jnp.arange(0, half, 2, dtype=F32) / half)
    ang_r = row[:, None] * freqs
    ang_c = col[:, None] * freqs
    cos = jnp.concatenate([jnp.cos(ang_r)] * 2 + [jnp.cos(ang_c)] * 2, axis=1)
    sin = jnp.concatenate([-jnp.sin(ang_r), jnp.sin(ang_r), -jnp.sin(ang_c), jnp.sin(ang_c)], axis=1)
    cos = jnp.concatenate([jnp.ones((c_len, HEAD), F32), cos], axis=0)
    sin = jnp.concatenate([jnp.zeros((c_len, HEAD), F32), sin], axis=0)
    return jnp.tile(cos, (1, 2)), jnp.tile(sin, (1, 2))


def kernel(x, c, ctx, c_ctx, w_mod, b_mod, norm_mix, norm_mlp, mlp_w1, mlp_w2, rwkv_mu, rwkv_wr, rwkv_wk, rwkv_wv,
           rwkv_wo, rwkv_w0, rwkv_w1, rwkv_w2, rwkv_a0, rwkv_a1, rwkv_a2, rwkv_g1, rwkv_g2, rwkv_k_k, rwkv_k_a,
           rwkv_r_k, rwkv_ln_w, rwkv_ln_b, attn_wqkv, attn_q_norm, attn_k_norm, attn_wo, final_norm):
    b, s, d = x.shape
    c_len = ctx.shape[1]
    assert d % (4 * 128) == 0 and d // HEAD <= SEG_W
    assert s % TM == 0 and c_len % TM == 0 and TM % GRID_W == 0 and TM % CHUNK == 0
    assert w_mod.shape[0] == 2 and rwkv_mu.shape[0] == 1 and attn_wqkv.shape[0] == 1
    ctx_tiles = c_len // TM
    row = lambda a: a.reshape(1, -1).astype(F32)
    bf = lambda a: a.astype(BF16)
    consts = _constants(d)

    rows = -(-(b + 1) // 8) * 8
    cvec = jnp.zeros((rows, d), F32).at[:b].set(c).at[b].set(c_ctx)
    m = _modulation(cvec, w_mod, b_mod)
    m = m.reshape(2, rows, 6, d)
    mods = [jnp.stack([jnp.broadcast_to(m[l, b][None], (b, 6, d)), m[l, :b]], axis=1) for l in range(2)]

    xall = jnp.concatenate([ctx, x], axis=1)

    p0 = dict(
        mu=rwkv_mu[0], wr=bf(rwkv_wr[0]), wk=bf(rwkv_wk[0]), wv=bf(rwkv_wv[0]), wo=bf(rwkv_wo[0]),
        g1=bf(rwkv_g1[0]), g2=bf(rwkv_g2[0]),
        w1=bf(jnp.concatenate([rwkv_w1[0, 0], rwkv_w1[0, 1]], axis=1)),
        w2=bf(_block_diag2(rwkv_w2[0, 0], rwkv_w2[0, 1])),
        w0=rwkv_w0[0].reshape(1, -1),
        a1=bf(jnp.concatenate([rwkv_a1[0, 0], rwkv_a1[0, 1]], axis=1)),
        a2=bf(_block_diag2(rwkv_a2[0, 0], rwkv_a2[0, 1])),
        a0=rwkv_a0[0].reshape(1, -1),
        k_k=row(rwkv_k_k[0]), k_a=row(rwkv_k_a[0]), r_k=row(rwkv_r_k[0]),
        ln_w=row(rwkv_ln_w[0]), ln_b=row(rwkv_ln_b[0]),
    )
    (v, g, bonus, a_f, r_f, b_f, k_f, a_b, r_b, b_b, k_b, dl_f, dl_b) = _proj0(
        xall, mods[0], row(norm_mix[0]), p0, consts, ctx_tiles)
    nc = (s + c_len) // CHUNK
    y_f = _wkv(False, c_len // CHUNK, a_f, r_f, b_f, k_f, v, dl_f.reshape(b, nc, d))
    y_b = _wkv(True, c_len // CHUNK, a_b, r_b, b_b, k_b, v, dl_b.reshape(b, nc, d))
    x1 = _readout0(xall, y_f, y_b, bonus, g, mods[0], p0, row(norm_mlp[0]), bf(mlp_w1[0]), bf(mlp_w2[0]), consts)

    cos, sin = _rope_tables(s, c_len)
    qn = jnp.tile(attn_q_norm[0], d // HEAD).reshape(1, -1)
    kn = jnp.tile(attn_k_norm[0], KV_HEADS).reshape(1, -1)
    q, k, vv = _qkv1(x1, mods[1], row(norm_mix[1]), bf(attn_wqkv[0]), qn, kn, cos, sin, consts)
    att = _attention(q, k, vv, ctx_tiles)
    return _out1(x1, att, mods[1], bf(attn_wo[0]), row(norm_mlp[1]), bf(mlp_w1[1]), bf(mlp_w2[1]),
                 row(final_norm), ctx_tiles)
```

```python
import functools
import math

import jax
import jax.numpy as jnp
from jax import lax
from jax.experimental import pallas as pl
from jax.experimental.pallas import tpu as pltpu

F32 = jnp.float32
BF16 = jnp.bfloat16

HEAD = 64
GRID_W = 64
KV_HEADS = 4
NORM_EPS = 1e-6
GN_EPS = 64e-5
ROPE_THETA = 10000.0
CHUNK = 64
TM = 256
SEG_W = 128
VMEM_LIMIT = 56 * 1024 * 1024
EXP_NEG_HALF = math.exp(-0.5)


def _bdot(a, b):
    return jnp.dot(a.astype(BF16), b.astype(BF16), preferred_element_type=F32)


def _split_hi_lo(x):
    hi = x.astype(BF16)
    lo = (x - hi.astype(F32)).astype(BF16)
    return hi, lo


def _seg_sum(x, seg, segt):
    hi, lo = _split_hi_lo(x)
    s = jnp.dot(hi, seg, preferred_element_type=F32) + jnp.dot(lo, seg, preferred_element_type=F32)
    s_hi, s_lo = _split_hi_lo(s)
    return jnp.dot(s_hi, segt, preferred_element_type=F32) + jnp.dot(s_lo, segt, preferred_element_type=F32)


def _norm_mod(x, gain, shift, scale):
    ms = jnp.mean(x * x, axis=-1, keepdims=True)
    y = x * lax.rsqrt(ms + NORM_EPS) * gain
    return y * (1.0 + scale) + shift


def _cparams(sem):
    return pltpu.CompilerParams(dimension_semantics=sem, vmem_limit_bytes=VMEM_LIMIT)


def _const_spec(shape):
    nd = len(shape)
    return pl.BlockSpec(shape, lambda *_: (0,) * nd)


def _mod_kernel(c_ref, w_ref, b_ref, o_ref):
    c = c_ref[...]
    s = c * jax.nn.sigmoid(c)
    s_hi, s_lo = _split_hi_lo(s)
    w = w_ref[0]
    w_hi, w_lo = _split_hi_lo(w)
    acc = jnp.dot(s_hi, w_hi, preferred_element_type=F32)
    acc += jnp.dot(s_hi, w_lo, preferred_element_type=F32)
    acc += jnp.dot(s_lo, w_hi, preferred_element_type=F32)
    o_ref[0] = acc + b_ref[0]


def _modulation(cvec, w_mod, b_mod):
    depth, d, n = w_mod.shape
    rows = cvec.shape[0]
    tn = 512
    return pl.pallas_call(
        _mod_kernel,
        grid=(depth, n // tn),
        in_specs=[
            pl.BlockSpec((rows, d), lambda l, j: (0, 0)),
            pl.BlockSpec((1, d, tn), lambda l, j: (l, 0, j)),
            pl.BlockSpec((1, 1, tn), lambda l, j: (l, 0, j)),
        ],
        out_specs=pl.BlockSpec((1, rows, tn), lambda l, j: (l, 0, j)),
        out_shape=jax.ShapeDtypeStruct((depth, rows, n), F32),
        compiler_params=_cparams(("parallel", "parallel")),
        name="adaln_modulation",
    )(cvec, w_mod, b_mod.reshape(depth, 1, n))


def _proj0_kernel(nt, xc_ref, xp_ref, xn_ref, mod_ref, gain_ref, mu_ref, wr_ref, wk_ref, wv_ref, g1_ref, g2_ref,
                  w1_ref, w2_ref, w0_ref, a1_ref, a2_ref, a0_ref, kk_ref, ka_ref, rk_ref, seg_ref, segt_ref,
                  trif_ref, trib_ref,
                  v_out, g_out, bonus_out, af_out, rf_out, bf_out, kf_out, ab_out, rb_out, bb_out, kb_out,
                  dlf_out, dlb_out, hs_ref):
    i = pl.program_id(1)
    d = xc_ref.shape[-1]
    q = d // 4
    gain = gain_ref[...]
    shift = mod_ref[0, 0, 0:1, :]
    scale = mod_ref[0, 0, 1:2, :]
    h = _norm_mod(xc_ref[0], gain, shift, scale)
    t_idx = lax.broadcasted_iota(jnp.int32, (TM, 1), 0)

    @pl.when(i == 0)
    def _():
        prev = jnp.where(t_idx == 0, 0.0, pltpu.roll(h[:, :2 * q], 1, 0))
        nxt = jnp.where(t_idx == TM - 1, 0.0, pltpu.roll(h[:, 2 * q:], TM - 1, 0))
        hs_ref[:, :2 * q] = prev
        hs_ref[:, 2 * q:] = nxt

    @pl.when(i > 0)
    def _():
        col = t_idx % GRID_W
        left = jnp.where(col == 0, 0.0, pltpu.roll(h[:, :q], 1, 0))
        right = jnp.where(col == GRID_W - 1, 0.0, pltpu.roll(h[:, q:2 * q], TM - 1, 0))
        hp = _norm_mod(xp_ref[0], gain, shift, scale)[:, 2 * q:3 * q]
        hn = _norm_mod(xn_ref[0], gain, shift, scale)[:, 3 * q:]
        hp = jnp.where(i > 1, hp, 0.0)
        hn = jnp.where(i < nt - 1, hn, 0.0)
        hs_ref[:, :q] = left
        hs_ref[:, q:2 * q] = right
        hs_ref[:, 2 * q:3 * q] = jnp.concatenate([hp, h[:TM - GRID_W, 2 * q:3 * q]], axis=0)
        hs_ref[:, 3 * q:] = jnp.concatenate([h[GRID_W:, 3 * q:], hn], axis=0)

    xx = hs_ref[...] - h
    mix = lambda j: (h + xx * mu_ref[j:j + 1, :]).astype(BF16)
    seg = seg_ref[...]
    segt = segt_ref[...]

    r = jnp.dot(mix(0), wr_ref[...], preferred_element_type=F32)
    k = jnp.dot(mix(2), wk_ref[...], preferred_element_type=F32)
    v = jnp.dot(mix(3), wv_ref[...], preferred_element_type=F32)
    g = _bdot(jax.nn.sigmoid(jnp.dot(mix(5), g1_ref[...], preferred_element_type=F32)), g2_ref[...])
    wl = _bdot(jnp.tanh(jnp.dot(mix(1), w1_ref[...], preferred_element_type=F32)), w2_ref[...]) + w0_ref[...]
    al = _bdot(jnp.dot(mix(4), a1_ref[...], preferred_element_type=F32), a2_ref[...]) + a0_ref[...]
    v_out[0] = v.astype(BF16)
    g_out[0] = g.astype(BF16)

    kk = k * kk_ref[...]
    kk = kk * lax.rsqrt(_seg_sum(kk * kk, seg, segt) + 1e-12)

    outs = ((af_out, rf_out, bf_out, kf_out, dlf_out, trif_ref), (ab_out, rb_out, bb_out, kb_out, dlb_out, trib_ref))
    ksum = None
    for dr, (a_o, r_o, b_o, k_o, dl_o, tri_ref) in enumerate(outs):
        e = EXP_NEG_HALF * jax.nn.sigmoid(wl[:, dr * d:(dr + 1) * d])
        a_lr = jax.nn.sigmoid(al[:, dr * d:(dr + 1) * d])
        k_dir = k * (1.0 + (a_lr - 1.0) * ka_ref[...])
        ksum = k_dir if ksum is None else ksum + k_dir
        e_hi, e_lo = _split_hi_lo(e)
        tri = tri_ref[...]
        c = jnp.dot(tri, e_hi, preferred_element_type=F32) + jnp.dot(tri, e_lo, preferred_element_type=F32)
        grow = jnp.exp(c)
        shrink = jnp.exp(-c)
        a_o[0] = (-kk * jnp.exp(e - c)).astype(BF16)
        r_o[0] = (r * shrink).astype(BF16)
        b_o[0] = (kk * a_lr * grow).astype(BF16)
        k_o[0] = (k_dir * grow).astype(BF16)
        last = CHUNK - 1 if dr == 0 else 0
        dl_o[0, 0] = jnp.concatenate(
            [shrink[n * CHUNK + last:n * CHUNK + last + 1, :] for n in range(TM // CHUNK)], axis=0)

    bonus_out[0] = (_seg_sum(r * ksum * rk_ref[...], seg, segt) * v).astype(BF16)


def _proj0(xall, mods, gain, p, consts, ctx_tiles):
    b, t, d = xall.shape
    nt = t // TM
    hb = TM // GRID_W
    nhb = t // GRID_W
    tok = lambda bb, i: (bb, i, 0)
    big = jax.ShapeDtypeStruct((b, t, d), BF16)
    dl = jax.ShapeDtypeStruct((b, nt, TM // CHUNK, d), F32)
    tile_spec = pl.BlockSpec((1, TM, d), tok)
    dl_spec = pl.BlockSpec((1, 1, TM // CHUNK, d), lambda bb, i: (bb, i, 0, 0))
    weights = [p["mu"], p["wr"], p["wk"], p["wv"], p["g1"], p["g2"], p["w1"], p["w2"], p["w0"],
               p["a1"], p["a2"], p["a0"], p["k_k"], p["k_a"], p["r_k"],
               consts["seg"], consts["segt"], consts["tri_f"], consts["tri_b"]]
    return pl.pallas_call(
        functools.partial(_proj0_kernel, nt),
        grid=(b, nt),
        in_specs=[
            tile_spec,
            pl.BlockSpec((1, GRID_W, d), lambda bb, i: (bb, jnp.maximum(i * hb - 1, 0), 0)),
            pl.BlockSpec((1, GRID_W, d), lambda bb, i: (bb, jnp.minimum((i + 1) * hb, nhb - 1), 0)),
            pl.BlockSpec((1, 1, 6, d), lambda bb, i: (bb, jnp.minimum(i, 1), 0, 0)),
            _const_spec(gain.shape),
        ] + [_const_spec(w.shape) for w in weights],
        out_specs=[tile_spec] * 11 + [dl_spec] * 2,
        out_shape=[big] * 11 + [dl] * 2,
        scratch_shapes=[pltpu.VMEM((TM, d), F32)],
        compiler_params=_cparams(("parallel", "parallel")),
        name="rwkv_project",
    )(xall, xall, xall, mods, gain, *weights)


_NT = (((1,), (1,)), ((), ()))
_TN = (((0,), (0,)), ((), ()))
PAIR = 2 * HEAD


def _pair_masks(rows):
    lane = lax.broadcasted_iota(jnp.int32, (rows, PAIR), 1)
    return lane < HEAD, lane >= HEAD


def _tri_mask(reverse, inclusive):
    row = lax.broadcasted_iota(jnp.int32, (CHUNK, PAIR), 0)
    col = lax.broadcasted_iota(jnp.int32, (CHUNK, PAIR), 1) % HEAD
    if inclusive:
        return (col >= row) if reverse else (col <= row)
    return (col > row) if reverse else (col < row)


def _wkv_intra_kernel(reverse, a_ref, b_ref, k_ref, v_ref, w_out, u_out):
    npairs = a_ref.shape[-1] // PAIR
    heads = [(p, s) for p in range(npairs) for s in (0, 1)]
    lo, hi = _pair_masks(CHUNK)
    half = (lo, hi)
    strict = _tri_mask(reverse, False)
    row = lax.broadcasted_iota(jnp.int32, (CHUNK, PAIR), 0)
    lane = lax.broadcasted_iota(jnp.int32, (CHUNK, PAIR), 1)
    eye_hi = (lane - HEAD == row).astype(F32)
    zrows = jnp.zeros((CHUNK, PAIR), BF16)

    def chunk(ci, carry):
        rows = pl.ds(pl.multiple_of(ci * CHUNK, CHUNK), CHUNK)
        psl = lambda p: slice(p * PAIR, (p + 1) * PAIR)
        a = [a_ref[0, rows, psl(p)] for p in range(npairs)]
        bk = [jnp.concatenate([b_ref[0, rows, psl(p)], k_ref[0, rows, psl(p)]], axis=0) for p in range(npairs)]
        v = [v_ref[0, rows, psl(p)] for p in range(npairs)]
        am = [jnp.where(half[s], a[p], 0) for p, s in heads]
        vm = [jnp.concatenate([zrows, jnp.where(half[s], v[p], 0)], axis=0) for p, s in heads]
        sc = [jnp.where(strict, lax.dot_general(am[i], bk[p], _NT, preferred_element_type=F32), 0.0)
              for i, (p, s) in enumerate(heads)]
        scb = [x.astype(BF16) for x in sc]
        x = [jnp.where(lo, t, eye_hi) for t in sc]
        for _ in range(6):
            xb = [t.astype(BF16) for t in x]
            res = [jnp.dot(t[:, :HEAD], t, preferred_element_type=F32) for t in xb]
            x = [r + jnp.where(lo, 0.0, t) for r, t in zip(res, x)]
        tinv = [t[:, HEAD:].astype(BF16) for t in x]
        akv = [jnp.dot(scb[i], vm[i], preferred_element_type=F32).astype(BF16) for i in range(len(heads))]
        wu = [jnp.dot(tinv[i], jnp.concatenate([am[i], akv[i]], axis=1), preferred_element_type=F32)
              for i in range(len(heads))]
        wsum = [wu[2 * p] + wu[2 * p + 1] for p in range(npairs)]
        w_out[0, rows, :] = jnp.concatenate([t[:, :PAIR] for t in wsum], axis=1).astype(BF16)
        u_out[0, rows, :] = jnp.concatenate([t[:, PAIR:] for t in wsum], axis=1)
        return carry

    lax.fori_loop(0, a_ref.shape[1] // CHUNK, chunk, 0)


def _wkv_intra(reverse, a, bm, k, v):
    b, t, d = a.shape
    tile = pl.BlockSpec((1, TM, d), lambda bb, i: (bb, i, 0))
    return pl.pallas_call(
        functools.partial(_wkv_intra_kernel, reverse),
        grid=(b, t // TM),
        in_specs=[tile] * 4,
        out_specs=[tile, tile],
        out_shape=[jax.ShapeDtypeStruct((b, t, d), BF16), jax.ShapeDtypeStruct((b, t, d), F32)],
        compiler_params=_cparams(("parallel", "parallel")),
        name="wkv_intra_bwd" if reverse else "wkv_intra_fwd",
    )(a, bm, k, v)


def _wkv_scan_kernel(reverse, w_ref, u_ref, r_ref, b_ref, k_ref, v_ref, dl_ref, y_ref, z_ref):
    npairs = w_ref.shape[-1] // PAIR
    heads = [(p, s) for p in range(npairs) for s in (0, 1)]
    lo, hi = _pair_masks(CHUNK)
    half = (lo, hi)
    lo2, hi2 = _pair_masks(2 * CHUNK)
    half2 = (lo2, hi2)
    incl = _tri_mask(reverse, True)
    brow = lax.broadcasted_iota(jnp.int32, (PAIR, PAIR), 0) // HEAD
    bcol = lax.broadcasted_iota(jnp.int32, (PAIR, PAIR), 1) // HEAD
    same_head = brow == bcol

    @pl.when(pl.program_id(1) == 0)
    def _():
        z_ref[...] = jnp.zeros(z_ref.shape, F32)

    z = [z_ref[p] for p in range(npairs)]
    nchunks = w_ref.shape[1] // CHUNK
    order = range(nchunks - 1, -1, -1) if reverse else range(nchunks)
    psl = lambda p: slice(p * PAIR, (p + 1) * PAIR)
    for c in order:
        rows = slice(c * CHUNK, (c + 1) * CHUNK)
        r = [r_ref[0, rows, psl(p)] for p in range(npairs)]
        v = [v_ref[0, rows, psl(p)] for p in range(npairs)]
        bk = [jnp.concatenate([b_ref[0, rows, psl(p)], k_ref[0, rows, psl(p)]], axis=0) for p in range(npairs)]
        wr = [jnp.concatenate([w_ref[0, rows, psl(p)], r[p]], axis=0) for p in range(npairs)]
        sr = [jnp.where(incl, lax.dot_general(jnp.where(half[s], r[p], 0), bk[p], _NT,
                                              preferred_element_type=F32), 0.0).astype(BF16) for p, s in heads]
        ws = [lax.dot_general(wr[p], z[p].astype(BF16), _NT, preferred_element_type=F32) for p in range(npairs)]
        uv = [jnp.concatenate([(ws[p][:CHUNK] + u_ref[0, rows, psl(p)]).astype(BF16), v[p]], axis=0)
              for p in range(npairs)]
        ys = [ws[p][CHUNK:]
              + jnp.dot(sr[2 * p], jnp.where(half2[0], uv[p], 0), preferred_element_type=F32)
              + jnp.dot(sr[2 * p + 1], jnp.where(half2[1], uv[p], 0), preferred_element_type=F32)
              for p in range(npairs)]
        upd = [lax.dot_general(uv[p], bk[p], _TN, preferred_element_type=F32) for p in range(npairs)]
        z = [(z[p] + jnp.where(same_head, upd[p], 0.0)) * dl_ref[0, 0, c:c + 1, psl(p)] for p in range(npairs)]
        y_ref[0, rows, :] = jnp.concatenate(ys, axis=1)
    for p in range(npairs):
        z_ref[p] = z[p]


def _wkv_scan(reverse, ctx_tiles, w, u, r, bm, k, v, dl):
    b, t, d = w.shape
    nt = t // TM
    if reverse:
        blk = lambda j: jnp.where(j < ctx_tiles, ctx_tiles - 1 - j, nt - 1 - j + ctx_tiles)
    else:
        blk = lambda j: j
    tile = pl.BlockSpec((1, TM, d), lambda bb, j: (bb, blk(j), 0))
    return pl.pallas_call(
        functools.partial(_wkv_scan_kernel, reverse),
        grid=(b, nt),
        in_specs=[tile] * 6 + [pl.BlockSpec((1, 1, TM // CHUNK, d), lambda bb, j: (bb, blk(j), 0, 0))],
        out_specs=tile,
        out_shape=jax.ShapeDtypeStruct((b, t, d), F32),
        scratch_shapes=[pltpu.VMEM((d // PAIR, PAIR, PAIR), F32)],
        compiler_params=_cparams(("parallel", "arbitrary")),
        name="wkv_scan_bwd" if reverse else "wkv_scan_fwd",
    )(w, u, r, bm, k, v, dl)


def _mlp(h, w1_ref, w2_ref):
    hb = h.astype(BF16)
    dff = w1_ref.shape[1]
    fc = 1024
    acc = None
    for j in range(dff // fc):
        a = jnp.dot(hb, w1_ref[:, j * fc:(j + 1) * fc], preferred_element_type=F32)
        a = jnp.square(jnp.maximum(a, 0.0)).astype(BF16)
        part = jnp.dot(a, w2_ref[j * fc:(j + 1) * fc, :], preferred_element_type=F32)
        acc = part if acc is None else acc + part
    return acc


def _readout0_kernel(x_ref, yf_ref, yb_ref, bonus_ref, g_ref, mod_ref, lnw_ref, lnb_ref, wo_ref, gain_ref,
                     w1_ref, w2_ref, seg_ref, segt_ref, o_ref):
    seg = seg_ref[...]
    segt = segt_ref[...]
    y = yf_ref[0] + yb_ref[0]
    mean = _seg_sum(y, seg, segt) * (1.0 / HEAD)
    dev = y - mean
    var = _seg_sum(dev * dev, seg, segt) * (1.0 / HEAD)
    yn = dev * lax.rsqrt(var + GN_EPS) * lnw_ref[...] + lnb_ref[...]
    mixed = (yn + bonus_ref[0].astype(F32)) * g_ref[0].astype(F32)
    o = _bdot(mixed, wo_ref[...])
    x1 = x_ref[0] + mod_ref[0, 0, 2:3, :] * o
    h2 = _norm_mod(x1, gain_ref[...], mod_ref[0, 0, 3:4, :], mod_ref[0, 0, 4:5, :])
    o_ref[0] = x1 + mod_ref[0, 0, 5:6, :] * _mlp(h2, w1_ref, w2_ref)


def _readout0(xall, yf, yb, bonus, g, mods, p, gain_mlp, w1, w2, consts):
    b, t, d = xall.shape
    nt = t // TM
    tile = pl.BlockSpec((1, TM, d), lambda bb, i: (bb, i, 0))
    weights = [p["ln_w"], p["ln_b"], p["wo"], gain_mlp, w1, w2, consts["seg"], consts["segt"]]
    return pl.pallas_call(
        _readout0_kernel,
        grid=(b, nt),
        in_specs=[tile] * 5 + [pl.BlockSpec((1, 1, 6, d), lambda bb, i: (bb, jnp.minimum(i, 1), 0, 0))]
        + [_const_spec(w.shape) for w in weights],
        out_specs=tile,
        out_shape=jax.ShapeDtypeStruct((b, t, d), F32),
        compiler_params=_cparams(("parallel", "parallel")),
        name="rwkv_readout_mlp",
    )(xall, yf, yb, bonus, g, mods, *weights)


def _rope(x, cos, sin_signed):
    lane = lax.broadcasted_iota(jnp.int32, (1, 128), 1)
    low = (lane & 16) == 0
    cols = []
    for j in range(x.shape[1] // 128):
        xj = x[:, j * 128:(j + 1) * 128]
        partner = jnp.where(low, pltpu.roll(xj, 128 - 16, 1), pltpu.roll(xj, 16, 1))
        cols.append(xj * cos + partner * sin_signed)
    return jnp.concatenate(cols, axis=1)


def _qkv1_kernel(x_ref, mod_ref, gain_ref, w_ref, qn_ref, kn_ref, cos_ref, sin_ref, seg_ref, segt_ref,
                 segk_ref, segkt_ref, q_out, k_out, v_out):
    d = x_ref.shape[-1]
    nk = KV_HEADS * HEAD
    h = _norm_mod(x_ref[0], gain_ref[...], mod_ref[0, 0, 0:1, :], mod_ref[0, 0, 1:2, :])
    qkv = _bdot(h, w_ref[...])
    cos = cos_ref[...]
    sin = sin_ref[...]
    q = qkv[:, :d]
    q = q * lax.rsqrt(_seg_sum(q * q, seg_ref[...], segt_ref[...]) * (1.0 / HEAD) + NORM_EPS) * qn_ref[...]
    q_out[0] = (_rope(q, cos, sin) * (HEAD ** -0.5)).astype(BF16)
    k = qkv[:, d:d + nk]
    k = k * lax.rsqrt(_seg_sum(k * k, segk_ref[...], segkt_ref[...]) * (1.0 / HEAD) + NORM_EPS) * kn_ref[...]
    k = _rope(k, cos, sin).astype(BF16)
    v = qkv[:, d + nk:].astype(BF16)
    for gi in range(KV_HEADS):
        k_out[0, gi] = k[:, gi * HEAD:(gi + 1) * HEAD]
        v_out[0, gi] = v[:, gi * HEAD:(gi + 1) * HEAD]


def _qkv1(x2, mods, gain, wqkv, qn, kn, cos, sin, consts):
    b, t, d = x2.shape
    nt = t // TM
    tile = pl.BlockSpec((1, TM, d), lambda bb, i: (bb, i, 0))
    kv_shape = jax.ShapeDtypeStruct((b, KV_HEADS, t, HEAD), BF16)
    kv_spec = pl.BlockSpec((1, KV_HEADS, TM, HEAD), lambda bb, i: (bb, 0, i, 0))
    tab = pl.BlockSpec((TM, 128), lambda bb, i: (i, 0))
    weights = [gain, wqkv, qn, kn]
    segs = [consts["seg"], consts["segt"], consts["segk"], consts["segkt"]]
    return pl.pallas_call(
        _qkv1_kernel,
        grid=(b, nt),
        in_specs=[tile, pl.BlockSpec((1, 1, 6, d), lambda bb, i: (bb, jnp.minimum(i, 1), 0, 0))]
        + [_const_spec(w.shape) for w in weights] + [tab, tab] + [_const_spec(w.shape) for w in segs],
        out_specs=[tile, kv_spec, kv_spec],
        out_shape=[jax.ShapeDtypeStruct((b, t, d), BF16), kv_shape, kv_shape],
        compiler_params=_cparams(("parallel", "parallel")),
        name="attn_qkv",
    )(x2, mods, *weights, cos, sin, *segs)


def _attn_kernel(q_ref, k_ref, v_ref, o_ref):
    k = k_ref[0, 0]
    v = v_ref[0, 0]
    nt_dims = (((1,), (1,)), ((), ()))
    outs = []
    for j in range(q_ref.shape[-1] // HEAD):
        qj = q_ref[0, :, j * HEAD:(j + 1) * HEAD]
        s = lax.dot_general(qj, k, nt_dims, preferred_element_type=F32)
        m = jnp.max(s, axis=-1, keepdims=True)
        p = jnp.exp(s - m)
        l = jnp.sum(p, axis=-1, keepdims=True)
        o = jnp.dot(p.astype(BF16), v, preferred_element_type=F32)
        outs.append(o / l)
    o_ref[0] = jnp.concatenate(outs, axis=1).astype(BF16)


def _attention(q, k, v, ctx_tiles):
    b, t, d = q.shape
    s = t - ctx_tiles * TM
    gw = d // KV_HEADS
    kv_spec = pl.BlockSpec((1, 1, t, HEAD), lambda bb, gi, i: (bb, gi, 0, 0))
    return pl.pallas_call(
        _attn_kernel,
        grid=(b, KV_HEADS, s // TM),
        in_specs=[pl.BlockSpec((1, TM, gw), lambda bb, gi, i: (bb, i + ctx_tiles, gi)), kv_spec, kv_spec],
        out_specs=pl.BlockSpec((1, TM, gw), lambda bb, gi, i: (bb, i, gi)),
        out_shape=jax.ShapeDtypeStruct((b, s, d), BF16),
        compiler_params=_cparams(("parallel", "parallel", "parallel")),
        name="gqa_attention",
    )(q, k, v)


def _out1_kernel(x_ref, a_ref, mod_ref, wo_ref, gain_ref, w1_ref, w2_ref, fin_ref, o_ref):
    o = jnp.dot(a_ref[0], wo_ref[...], preferred_element_type=F32)
    x1 = x_ref[0] + mod_ref[0, 0, 2:3, :] * o
    h2 = _norm_mod(x1, gain_ref[...], mod_ref[0, 0, 3:4, :], mod_ref[0, 0, 4:5, :])
    x2 = x1 + mod_ref[0, 0, 5:6, :] * _mlp(h2, w1_ref, w2_ref)
    ms = jnp.mean(x2 * x2, axis=-1, keepdims=True)
    o_ref[0] = x2 * lax.rsqrt(ms + NORM_EPS) * fin_ref[...]


def _out1(x2, att, mods, wo, gain_mlp, w1, w2, fin, ctx_tiles):
    b, s, d = att.shape
    weights = [wo, gain_mlp, w1, w2, fin]
    return pl.pallas_call(
        _out1_kernel,
        grid=(b, s // TM),
        in_specs=[pl.BlockSpec((1, TM, d), lambda bb, i: (bb, i + ctx_tiles, 0)),
                  pl.BlockSpec((1, TM, d), lambda bb, i: (bb, i, 0)),
                  pl.BlockSpec((1, 1, 6, d), lambda bb, i: (bb, 1, 0, 0))]
        + [_const_spec(w.shape) for w in weights],
        out_specs=pl.BlockSpec((1, TM, d), lambda bb, i: (bb, i, 0)),
        out_shape=jax.ShapeDtypeStruct((b, s, d), F32),
        compiler_params=_cparams(("parallel", "parallel")),
        name="attn_out_mlp_final",
    )(x2, att, mods, *weights)


def _block_diag2(a, b):
    za = jnp.zeros((a.shape[0], b.shape[1]), a.dtype)
    zb = jnp.zeros((b.shape[0], a.shape[1]), b.dtype)
    return jnp.concatenate([jnp.concatenate([a, za], 1), jnp.concatenate([zb, b], 1)], 0)


def _constants(d):
    lane = jnp.arange(d)[:, None] // HEAD
    seg = (lane == jnp.arange(SEG_W)[None, :]).astype(BF16)
    nk = KV_HEADS * HEAD
    segk = seg[:nk]
    t = jnp.arange(TM)
    same = (t[:, None] // CHUNK) == (t[None, :] // CHUNK)
    tri_f = (same & (t[None, :] <= t[:, None])).astype(BF16)
    tri_b = (same & (t[None, :] >= t[:, None])).astype(BF16)
    return dict(seg=seg, segt=seg.T, segk=segk, segkt=segk.T, tri_f=tri_f, tri_b=tri_b)


def _rope_tables(s, c_len):
    tok = jnp.arange(s)
    row = (tok // GRID_W).astype(F32)
    col = (tok % GRID_W).astype(F32)
    half = HEAD // 2
    freqs = ROPE_THETA ** (-jnp.arange(0, half, 2, dtype=F32) / half)
    ang_r = row[:, None] * freqs
    ang_c = col[:, None] * freqs
    cos = jnp.concatenate([jnp.cos(ang_r)] * 2 + [jnp.cos(ang_c)] * 2, axis=1)
    sin = jnp.concatenate([-jnp.sin(ang_r), jnp.sin(ang_r), -jnp.sin(ang_c), jnp.sin(ang_c)], axis=1)
    cos = jnp.concatenate([jnp.ones((c_len, HEAD), F32), cos], axis=0)
    sin = jnp.concatenate([jnp.zeros((c_len, HEAD), F32), sin], axis=0)
    return jnp.tile(cos, (1, 2)), jnp.tile(sin, (1, 2))


def kernel(x, c, ctx, c_ctx, w_mod, b_mod, norm_mix, norm_mlp, mlp_w1, mlp_w2, rwkv_mu, rwkv_wr, rwkv_wk, rwkv_wv,
           rwkv_wo, rwkv_w0, rwkv_w1, rwkv_w2, rwkv_a0, rwkv_a1, rwkv_a2, rwkv_g1, rwkv_g2, rwkv_k_k, rwkv_k_a,
           rwkv_r_k, rwkv_ln_w, rwkv_ln_b, attn_wqkv, attn_q_norm, attn_k_norm, attn_wo, final_norm):
    b, s, d = x.shape
    c_len = ctx.shape[1]
    assert d % (4 * 128) == 0 and d // HEAD <= SEG_W
    assert s % TM == 0 and c_len % TM == 0 and TM % GRID_W == 0 and TM % CHUNK == 0
    assert w_mod.shape[0] == 2 and rwkv_mu.shape[0] == 1 and attn_wqkv.shape[0] == 1
    ctx_tiles = c_len // TM
    row = lambda a: a.reshape(1, -1).astype(F32)
    bf = lambda a: a.astype(BF16)
    consts = _constants(d)

    rows = -(-(b + 1) // 8) * 8
    cvec = jnp.zeros((rows, d), F32).at[:b].set(c).at[b].set(c_ctx)
    m = _modulation(cvec, w_mod, b_mod)
    m = m.reshape(2, rows, 6, d)
    mods = [jnp.stack([jnp.broadcast_to(m[l, b][None], (b, 6, d)), m[l, :b]], axis=1) for l in range(2)]

    xall = jnp.concatenate([ctx, x], axis=1)

    p0 = dict(
        mu=rwkv_mu[0], wr=bf(rwkv_wr[0]), wk=bf(rwkv_wk[0]), wv=bf(rwkv_wv[0]), wo=bf(rwkv_wo[0]),
        g1=bf(rwkv_g1[0]), g2=bf(rwkv_g2[0]),
        w1=bf(jnp.concatenate([rwkv_w1[0, 0], rwkv_w1[0, 1]], axis=1)),
        w2=bf(_block_diag2(rwkv_w2[0, 0], rwkv_w2[0, 1])),
        w0=rwkv_w0[0].reshape(1, -1),
        a1=bf(jnp.concatenate([rwkv_a1[0, 0], rwkv_a1[0, 1]], axis=1)),
        a2=bf(_block_diag2(rwkv_a2[0, 0], rwkv_a2[0, 1])),
        a0=rwkv_a0[0].reshape(1, -1),
        k_k=row(rwkv_k_k[0]), k_a=row(rwkv_k_a[0]), r_k=row(rwkv_r_k[0]),
        ln_w=row(rwkv_ln_w[0]), ln_b=row(rwkv_ln_b[0]),
    )
    (v, g, bonus, a_f, r_f, b_f, k_f, a_b, r_b, b_b, k_b, dl_f, dl_b) = _proj0(
        xall, mods[0], row(norm_mix[0]), p0, consts, ctx_tiles)
    w_f, u_f = _wkv_intra(False, a_f, b_f, k_f, v)
    w_b, u_b = _wkv_intra(True, a_b, b_b, k_b, v)
    y_f = _wkv_scan(False, ctx_tiles, w_f, u_f, r_f, b_f, k_f, v, dl_f)
    y_b = _wkv_scan(True, ctx_tiles, w_b, u_b, r_b, b_b, k_b, v, dl_b)
    x1 = _readout0(xall, y_f, y_b, bonus, g, mods[0], p0, row(norm_mlp[0]), bf(mlp_w1[0]), bf(mlp_w2[0]), consts)

    cos, sin = _rope_tables(s, c_len)
    qn = jnp.tile(attn_q_norm[0], d // HEAD).reshape(1, -1)
    kn = jnp.tile(attn_k_norm[0], KV_HEADS).reshape(1, -1)
    q, k, vv = _qkv1(x1, mods[1], row(norm_mix[1]), bf(attn_wqkv[0]), qn, kn, cos, sin, consts)
    att = _attention(q, k, vv, ctx_tiles)
    return _out1(x1, att, mods[1], bf(attn_wo[0]), row(norm_mlp[1]), bf(mlp_w1[1]), bf(mlp_w2[1]),
                 row(final_norm), ctx_tiles)
```

```python
import functools
import math

import jax
import jax.numpy as jnp
from jax import lax
from jax.experimental import pallas as pl
from jax.experimental.pallas import tpu as pltpu

F32 = jnp.float32
BF16 = jnp.bfloat16

HEAD = 64
GRID_W = 64
KV_HEADS = 4
NORM_EPS = 1e-6
GN_EPS = 64e-5
ROPE_THETA = 10000.0
CHUNK = 64
TM = 256
SEG_W = 128
VMEM_LIMIT = 56 * 1024 * 1024
EXP_NEG_HALF = math.exp(-0.5)
LOG2E = math.log2(math.e)


def _bdot(a, b):
    return jnp.dot(a.astype(BF16), b.astype(BF16), preferred_element_type=F32)


def _split_hi_lo(x):
    hi = x.astype(BF16)
    lo = (x - hi.astype(F32)).astype(BF16)
    return hi, lo


def _seg_sum(x, seg, segt, split_input=True):
    if split_input:
        hi, lo = _split_hi_lo(x)
        s = jnp.dot(hi, seg, preferred_element_type=F32) + jnp.dot(lo, seg, preferred_element_type=F32)
    else:
        s = jnp.dot(x.astype(BF16), seg, preferred_element_type=F32)
    s_hi, s_lo = _split_hi_lo(s)
    return jnp.dot(s_hi, segt, preferred_element_type=F32) + jnp.dot(s_lo, segt, preferred_element_type=F32)


def _sigmoid(x):
    return 0.5 * jnp.tanh(0.5 * x) + 0.5


def _norm_mod(x, gain, shift, scale):
    ms = jnp.mean(x * x, axis=-1, keepdims=True)
    return x * lax.rsqrt(ms + NORM_EPS) * (gain * (1.0 + scale)) + shift


def _cparams(sem):
    return pltpu.CompilerParams(dimension_semantics=sem, vmem_limit_bytes=VMEM_LIMIT)


def _const_spec(shape):
    nd = len(shape)
    return pl.BlockSpec(shape, lambda *_: (0,) * nd)


def _mod_kernel(c_ref, w_ref, b_ref, o_ref):
    c = c_ref[...]
    s = c * jax.nn.sigmoid(c)
    s_hi, s_lo = _split_hi_lo(s)
    w = w_ref[0]
    w_hi, w_lo = _split_hi_lo(w)
    acc = jnp.dot(s_hi, w_hi, preferred_element_type=F32)
    acc += jnp.dot(s_hi, w_lo, preferred_element_type=F32)
    acc += jnp.dot(s_lo, w_hi, preferred_element_type=F32)
    o_ref[0] = acc + b_ref[0]


def _modulation(cvec, w_mod, b_mod):
    depth, d, n = w_mod.shape
    rows = cvec.shape[0]
    tn = 512
    return pl.pallas_call(
        _mod_kernel,
        grid=(depth, n // tn),
        in_specs=[
            pl.BlockSpec((rows, d), lambda l, j: (0, 0)),
            pl.BlockSpec((1, d, tn), lambda l, j: (l, 0, j)),
            pl.BlockSpec((1, 1, tn), lambda l, j: (l, 0, j)),
        ],
        out_specs=pl.BlockSpec((1, rows, tn), lambda l, j: (l, 0, j)),
        out_shape=jax.ShapeDtypeStruct((depth, rows, n), F32),
        compiler_params=_cparams(("parallel", "parallel")),
        name="adaln_modulation",
    )(cvec, w_mod, b_mod.reshape(depth, 1, n))


def _proj0_kernel(nt, xc_ref, xp_ref, xn_ref, mod_ref, gain_ref, mu_ref, wr_ref, wk_ref, wv_ref, g1_ref, g2_ref,
                  w1_ref, w2_ref, w0_ref, a1_ref, a2_ref, a0_ref, kk_ref, ka_ref, rk_ref, seg_ref, segt_ref,
                  trif_ref, trib_ref,
                  v_out, g_out, bonus_out, af_out, rf_out, bf_out, kf_out, ab_out, rb_out, bb_out, kb_out,
                  dlf_out, dlb_out, hs_ref):
    i = pl.program_id(1)
    d = xc_ref.shape[-1]
    q = d // 4
    gain = gain_ref[...]
    shift = mod_ref[0, 0, 0:1, :]
    scale = mod_ref[0, 0, 1:2, :]
    h = _norm_mod(xc_ref[0], gain, shift, scale)
    t_idx = lax.broadcasted_iota(jnp.int32, (TM, 1), 0)

    @pl.when(i == 0)
    def _():
        prev = jnp.where(t_idx == 0, 0.0, pltpu.roll(h[:, :2 * q], 1, 0))
        nxt = jnp.where(t_idx == TM - 1, 0.0, pltpu.roll(h[:, 2 * q:], TM - 1, 0))
        hs_ref[:, :2 * q] = prev
        hs_ref[:, 2 * q:] = nxt

    @pl.when(i > 0)
    def _():
        col = t_idx % GRID_W
        left = jnp.where(col == 0, 0.0, pltpu.roll(h[:, :q], 1, 0))
        right = jnp.where(col == GRID_W - 1, 0.0, pltpu.roll(h[:, q:2 * q], TM - 1, 0))
        hp = _norm_mod(xp_ref[0], gain, shift, scale)[:, 2 * q:3 * q]
        hn = _norm_mod(xn_ref[0], gain, shift, scale)[:, 3 * q:]
        hp = jnp.where(i > 1, hp, 0.0)
        hn = jnp.where(i < nt - 1, hn, 0.0)
        hs_ref[:, :q] = left
        hs_ref[:, q:2 * q] = right
        hs_ref[:, 2 * q:3 * q] = jnp.concatenate([hp, h[:TM - GRID_W, 2 * q:3 * q]], axis=0)
        hs_ref[:, 3 * q:] = jnp.concatenate([h[GRID_W:, 3 * q:], hn], axis=0)

    hb = h.astype(BF16)
    xxb = (hs_ref[...] - h).astype(BF16)
    mub = mu_ref[...].astype(BF16)
    mix = lambda j: hb + xxb * mub[j:j + 1, :]
    seg = seg_ref[...]
    segt = segt_ref[...]

    r = jnp.dot(mix(0), wr_ref[...], preferred_element_type=F32)
    k = jnp.dot(mix(2), wk_ref[...], preferred_element_type=F32)
    v = jnp.dot(mix(3), wv_ref[...], preferred_element_type=F32)
    g = _bdot(_sigmoid(jnp.dot(mix(5), g1_ref[...], preferred_element_type=F32)), g2_ref[...])
    wl = _bdot(jnp.tanh(jnp.dot(mix(1), w1_ref[...], preferred_element_type=F32)), w2_ref[...]) + w0_ref[...]
    al = _bdot(jnp.dot(mix(4), a1_ref[...], preferred_element_type=F32), a2_ref[...]) + a0_ref[...]
    v_out[0] = v.astype(BF16)
    g_out[0] = g.astype(BF16)

    kk = k * kk_ref[...]
    kk = kk * lax.rsqrt(_seg_sum(kk * kk, seg, segt, split_input=False) + 1e-12)

    outs = ((af_out, rf_out, bf_out, kf_out, dlf_out, trif_ref), (ab_out, rb_out, bb_out, kb_out, dlb_out, trib_ref))
    ksum = None
    for dr, (a_o, r_o, b_o, k_o, dl_o, tri_ref) in enumerate(outs):
        e = EXP_NEG_HALF * _sigmoid(wl[:, dr * d:(dr + 1) * d])
        a_lr = _sigmoid(al[:, dr * d:(dr + 1) * d])
        k_dir = k * (1.0 + (a_lr - 1.0) * ka_ref[...])
        ksum = k_dir if ksum is None else ksum + k_dir
        e_hi, e_lo = _split_hi_lo(e)
        tri = tri_ref[...]
        c = jnp.dot(tri, e_hi, preferred_element_type=F32) + jnp.dot(tri, e_lo, preferred_element_type=F32)
        grow = jnp.exp(c)
        shrink = jnp.exp(-c)
        a_o[0] = (-kk * jnp.exp(e - c)).astype(BF16)
        r_o[0] = (r * shrink).astype(BF16)
        b_o[0] = (kk * a_lr * grow).astype(BF16)
        k_o[0] = (k_dir * grow).astype(BF16)
        last = CHUNK - 1 if dr == 0 else 0
        dl_o[0, 0] = jnp.concatenate(
            [shrink[n * CHUNK + last:n * CHUNK + last + 1, :] for n in range(TM // CHUNK)], axis=0)

    bonus_out[0] = (_seg_sum(r * ksum * rk_ref[...], seg, segt) * v).astype(BF16)


def _proj0(xall, mods, gain, p, consts, ctx_tiles):
    b, t, d = xall.shape
    nt = t // TM
    hb = TM // GRID_W
    nhb = t // GRID_W
    tok = lambda bb, i: (bb, i, 0)
    big = jax.ShapeDtypeStruct((b, t, d), BF16)
    dl = jax.ShapeDtypeStruct((b, nt, TM // CHUNK, d), F32)
    tile_spec = pl.BlockSpec((1, TM, d), tok)
    dl_spec = pl.BlockSpec((1, 1, TM // CHUNK, d), lambda bb, i: (bb, i, 0, 0))
    weights = [p["mu"], p["wr"], p["wk"], p["wv"], p["g1"], p["g2"], p["w1"], p["w2"], p["w0"],
               p["a1"], p["a2"], p["a0"], p["k_k"], p["k_a"], p["r_k"],
               consts["seg"], consts["segt"], consts["tri_f"], consts["tri_b"]]
    return pl.pallas_call(
        functools.partial(_proj0_kernel, nt),
        grid=(b, nt),
        in_specs=[
            tile_spec,
            pl.BlockSpec((1, GRID_W, d), lambda bb, i: (bb, jnp.maximum(i * hb - 1, 0), 0)),
            pl.BlockSpec((1, GRID_W, d), lambda bb, i: (bb, jnp.minimum((i + 1) * hb, nhb - 1), 0)),
            pl.BlockSpec((1, 1, 6, d), lambda bb, i: (bb, jnp.minimum(i, 1), 0, 0)),
            _const_spec(gain.shape),
        ] + [_const_spec(w.shape) for w in weights],
        out_specs=[tile_spec] * 11 + [dl_spec] * 2,
        out_shape=[big] * 11 + [dl] * 2,
        scratch_shapes=[pltpu.VMEM((TM, d), F32)],
        compiler_params=_cparams(("parallel", "parallel")),
        name="rwkv_project",
    )(xall, xall, xall, mods, gain, *weights)


_NT = (((1,), (1,)), ((), ()))
_TN = (((0,), (0,)), ((), ()))
PAIR = 2 * HEAD


def _pair_masks(rows):
    lane = lax.broadcasted_iota(jnp.int32, (rows, PAIR), 1)
    return lane < HEAD, lane >= HEAD


def _tri_mask(reverse, inclusive):
    row = lax.broadcasted_iota(jnp.int32, (CHUNK, PAIR), 0)
    col = lax.broadcasted_iota(jnp.int32, (CHUNK, PAIR), 1) % HEAD
    if inclusive:
        return (col >= row) if reverse else (col <= row)
    return (col > row) if reverse else (col < row)


def _wkv_intra_kernel(reverse, a_ref, b_ref, k_ref, v_ref, w_out, u_out):
    npairs = a_ref.shape[-1] // PAIR
    lo, hi = _pair_masks(CHUNK)
    half = (lo, hi)
    strict = _tri_mask(reverse, False)
    row = lax.broadcasted_iota(jnp.int32, (CHUNK, PAIR), 0)
    lane = lax.broadcasted_iota(jnp.int32, (CHUNK, PAIR), 1)
    eye_hi = (lane - HEAD == row).astype(F32)
    zrows = jnp.zeros((CHUNK, PAIR), BF16)

    per_iter = 2

    def chunk(ci, carry):
        rows = [pl.ds(pl.multiple_of((ci * per_iter + n) * CHUNK, CHUNK), CHUNK) for n in range(per_iter)]
        psl = lambda p: slice(p * PAIR, (p + 1) * PAIR)
        slabs = [(n, p) for n in range(per_iter) for p in range(npairs)]
        chains = [(g, s) for g in range(len(slabs)) for s in (0, 1)]
        a = [a_ref[0, rows[n], psl(p)] for n, p in slabs]
        bk = [jnp.concatenate([b_ref[0, rows[n], psl(p)], k_ref[0, rows[n], psl(p)]], axis=0) for n, p in slabs]
        v = [v_ref[0, rows[n], psl(p)] for n, p in slabs]
        am = [jnp.where(half[s], a[g], 0) for g, s in chains]
        vm = [jnp.concatenate([zrows, jnp.where(half[s], v[g], 0)], axis=0) for g, s in chains]
        sc = [jnp.where(strict, lax.dot_general(am[i], bk[g], _NT, preferred_element_type=F32), 0.0)
              for i, (g, s) in enumerate(chains)]
        scb = [x.astype(BF16) for x in sc]
        x = [jnp.where(lo, t, eye_hi) for t in sc]
        for _ in range(6):
            xb = [t.astype(BF16) for t in x]
            res = [jnp.dot(t[:, :HEAD], t, preferred_element_type=F32) for t in xb]
            x = [r + jnp.where(lo, 0.0, t) for r, t in zip(res, x)]
        tinv = [t[:, HEAD:].astype(BF16) for t in x]
        akv = [jnp.dot(scb[i], vm[i], preferred_element_type=F32).astype(BF16) for i in range(len(chains))]
        wu = [jnp.dot(tinv[i], jnp.concatenate([am[i], akv[i]], axis=1), preferred_element_type=F32)
              for i in range(len(chains))]
        wsum = [wu[2 * g] + wu[2 * g + 1] for g in range(len(slabs))]
        for n in range(per_iter):
            mine = wsum[n * npairs:(n + 1) * npairs]
            w_out[0, rows[n], :] = jnp.concatenate([t[:, :PAIR] for t in mine], axis=1).astype(BF16)
            u_out[0, rows[n], :] = jnp.concatenate([t[:, PAIR:] for t in mine], axis=1)
        return carry

    lax.fori_loop(0, a_ref.shape[1] // (CHUNK * per_iter), chunk, 0)


def _wkv_intra(reverse, a, bm, k, v):
    b, t, d = a.shape
    tile = pl.BlockSpec((1, TM, d), lambda bb, i: (bb, i, 0))
    return pl.pallas_call(
        functools.partial(_wkv_intra_kernel, reverse),
        grid=(b, t // TM),
        in_specs=[tile] * 4,
        out_specs=[tile, tile],
        out_shape=[jax.ShapeDtypeStruct((b, t, d), BF16), jax.ShapeDtypeStruct((b, t, d), F32)],
        compiler_params=_cparams(("parallel", "parallel")),
        name="wkv_intra_bwd" if reverse else "wkv_intra_fwd",
    )(a, bm, k, v)


def _wkv_scan_kernel(wf, uf, rf, bf, kf, vf, dlf, wb, ub, rb, bb, kb, vb, dlb, yf_ref, yb_ref, z_ref):
    dirs = ((False, wf, uf, rf, bf, kf, vf, dlf, yf_ref), (True, wb, ub, rb, bb, kb, vb, dlb, yb_ref))
    npairs = wf.shape[-1] // PAIR
    nchunks = wf.shape[1] // CHUNK
    slabs = [(dd, p) for dd in range(2) for p in range(npairs)]
    chains = [(g, s) for g in range(len(slabs)) for s in (0, 1)]
    half = _pair_masks(CHUNK)
    half2 = _pair_masks(2 * CHUNK)
    incl = (_tri_mask(False, True), _tri_mask(True, True))
    brow = lax.broadcasted_iota(jnp.int32, (PAIR, PAIR), 0) // HEAD
    bcol = lax.broadcasted_iota(jnp.int32, (PAIR, PAIR), 1) // HEAD
    same_head = brow == bcol
    psl = lambda p: slice(p * PAIR, (p + 1) * PAIR)

    @pl.when(pl.program_id(1) == 0)
    def _():
        z_ref[...] = jnp.zeros(z_ref.shape, F32)

    z = [z_ref[dd, p] for dd, p in slabs]
    for n in range(nchunks):
        rows = [slice((nchunks - 1 - n) * CHUNK, (nchunks - n) * CHUNK) if dirs[dd][0]
                else slice(n * CHUNK, (n + 1) * CHUNK) for dd, _ in slabs]
        cidx = [(nchunks - 1 - n) if dirs[dd][0] else n for dd, _ in slabs]
        ld = lambda pos, g: dirs[slabs[g][0]][pos][0, rows[g], psl(slabs[g][1])]
        ng = len(slabs)
        r = [ld(3, g) for g in range(ng)]
        v = [ld(6, g) for g in range(ng)]
        bk = [jnp.concatenate([ld(4, g), ld(5, g)], axis=0) for g in range(ng)]
        wr = [jnp.concatenate([ld(1, g), r[g]], axis=0) for g in range(ng)]
        sr = [jnp.where(incl[slabs[g][0]],
                        lax.dot_general(jnp.where(half[s], r[g], 0), bk[g], _NT, preferred_element_type=F32),
                        0.0).astype(BF16) for g, s in chains]
        ws = [lax.dot_general(wr[g], z[g].astype(BF16), _NT, preferred_element_type=F32) for g in range(ng)]
        uv = [jnp.concatenate([(ws[g][:CHUNK] + ld(2, g)).astype(BF16), v[g]], axis=0) for g in range(ng)]
        ys = [ws[g][CHUNK:]
              + jnp.dot(sr[2 * g], jnp.where(half2[0], uv[g], 0), preferred_element_type=F32)
              + jnp.dot(sr[2 * g + 1], jnp.where(half2[1], uv[g], 0), preferred_element_type=F32)
              for g in range(ng)]
        upd = [lax.dot_general(uv[g], bk[g], _TN, preferred_element_type=F32) for g in range(ng)]
        z = [(z[g] + jnp.where(same_head, upd[g], 0.0))
             * dirs[slabs[g][0]][7][0, 0, cidx[g]:cidx[g] + 1, psl(slabs[g][1])] for g in range(ng)]
        for dd in range(2):
            dirs[dd][8][0, rows[dd * npairs], :] = jnp.concatenate(ys[dd * npairs:(dd + 1) * npairs], axis=1)
    for g, (dd, p) in enumerate(slabs):
        z_ref[dd, p] = z[g]


def _wkv_scan(ctx_tiles, fwd, bwd):
    b, t, d = fwd[0].shape
    nt = t // TM
    blk_b = lambda j: jnp.where(j < ctx_tiles, ctx_tiles - 1 - j, nt - 1 - j + ctx_tiles)
    specs = []
    for blk in (lambda j: j, blk_b):
        tile = pl.BlockSpec((1, TM, d), lambda bb, j, blk=blk: (bb, blk(j), 0))
        dl_spec = pl.BlockSpec((1, 1, TM // CHUNK, d), lambda bb, j, blk=blk: (bb, blk(j), 0, 0))
        specs.append((tile, dl_spec))
    return pl.pallas_call(
        _wkv_scan_kernel,
        grid=(b, nt),
        in_specs=[specs[0][0]] * 6 + [specs[0][1]] + [specs[1][0]] * 6 + [specs[1][1]],
        out_specs=[specs[0][0], specs[1][0]],
        out_shape=[jax.ShapeDtypeStruct((b, t, d), F32)] * 2,
        scratch_shapes=[pltpu.VMEM((2, d // PAIR, PAIR, PAIR), F32)],
        compiler_params=_cparams(("parallel", "arbitrary")),
        name="wkv_scan",
    )(*fwd, *bwd)


def _mlp(h, w1_ref, w2_ref):
    hb = h.astype(BF16)
    dff = w1_ref.shape[1]
    fc = 1024
    acc = None
    for j in range(dff // fc):
        a = jnp.dot(hb, w1_ref[:, j * fc:(j + 1) * fc], preferred_element_type=F32)
        a = jnp.square(jnp.maximum(a, 0.0)).astype(BF16)
        part = jnp.dot(a, w2_ref[j * fc:(j + 1) * fc, :], preferred_element_type=F32)
        acc = part if acc is None else acc + part
    return acc


def _readout0_kernel(x_ref, yf_ref, yb_ref, bonus_ref, g_ref, mod_ref, lnw_ref, lnb_ref, wo_ref, gain_ref,
                     w1_ref, w2_ref, seg_ref, segt_ref, o_ref):
    seg = seg_ref[...]
    segt = segt_ref[...]
    y = yf_ref[0] + yb_ref[0]
    mean = _seg_sum(y, seg, segt) * (1.0 / HEAD)
    dev = y - mean
    var = _seg_sum(dev * dev, seg, segt) * (1.0 / HEAD)
    yn = dev * lax.rsqrt(var + GN_EPS) * lnw_ref[...] + lnb_ref[...]
    mixed = (yn + bonus_ref[0].astype(F32)) * g_ref[0].astype(F32)
    o = _bdot(mixed, wo_ref[...])
    x1 = x_ref[0] + mod_ref[0, 0, 2:3, :] * o
    h2 = _norm_mod(x1, gain_ref[...], mod_ref[0, 0, 3:4, :], mod_ref[0, 0, 4:5, :])
    o_ref[0] = x1 + mod_ref[0, 0, 5:6, :] * _mlp(h2, w1_ref, w2_ref)


def _readout0(xall, yf, yb, bonus, g, mods, p, gain_mlp, w1, w2, consts):
    b, t, d = xall.shape
    nt = t // TM
    tile = pl.BlockSpec((1, TM, d), lambda bb, i: (bb, i, 0))
    weights = [p["ln_w"], p["ln_b"], p["wo"], gain_mlp, w1, w2, consts["seg"], consts["segt"]]
    return pl.pallas_call(
        _readout0_kernel,
        grid=(b, nt),
        in_specs=[tile] * 5 + [pl.BlockSpec((1, 1, 6, d), lambda bb, i: (bb, jnp.minimum(i, 1), 0, 0))]
        + [_const_spec(w.shape) for w in weights],
        out_specs=tile,
        out_shape=jax.ShapeDtypeStruct((b, t, d), F32),
        compiler_params=_cparams(("parallel", "parallel")),
        name="rwkv_readout_mlp",
    )(xall, yf, yb, bonus, g, mods, *weights)


def _rope(x, cos, sin_signed):
    lane = lax.broadcasted_iota(jnp.int32, (1, 128), 1)
    low = (lane & 16) == 0
    cols = []
    for j in range(x.shape[1] // 128):
        xj = x[:, j * 128:(j + 1) * 128]
        partner = jnp.where(low, pltpu.roll(xj, 128 - 16, 1), pltpu.roll(xj, 16, 1))
        cols.append(xj * cos + partner * sin_signed)
    return jnp.concatenate(cols, axis=1)


def _qkv1_kernel(x_ref, mod_ref, gain_ref, w_ref, qn_ref, kn_ref, cos_ref, sin_ref, seg_ref, segt_ref,
                 segk_ref, segkt_ref, q_out, k_out, v_out):
    d = x_ref.shape[-1]
    nk = KV_HEADS * HEAD
    h = _norm_mod(x_ref[0], gain_ref[...], mod_ref[0, 0, 0:1, :], mod_ref[0, 0, 1:2, :])
    qkv = _bdot(h, w_ref[...])
    cos = cos_ref[...]
    sin = sin_ref[...]
    q = qkv[:, :d]
    q = q * lax.rsqrt(_seg_sum(q * q, seg_ref[...], segt_ref[...]) * (1.0 / HEAD) + NORM_EPS) * qn_ref[...]
    q_out[0] = (_rope(q, cos, sin) * (LOG2E * HEAD ** -0.5)).astype(BF16)
    k = qkv[:, d:d + nk]
    k = k * lax.rsqrt(_seg_sum(k * k, segk_ref[...], segkt_ref[...]) * (1.0 / HEAD) + NORM_EPS) * kn_ref[...]
    k = _rope(k, cos, sin).astype(BF16)
    v = qkv[:, d + nk:].astype(BF16)
    ones = jnp.ones((v.shape[0], HEAD), BF16)
    for gi in range(KV_HEADS):
        k_out[0, gi] = k[:, gi * HEAD:(gi + 1) * HEAD]
        v_out[0, gi] = jnp.concatenate([v[:, gi * HEAD:(gi + 1) * HEAD], ones], axis=1)


def _qkv1(x2, mods, gain, wqkv, qn, kn, cos, sin, consts):
    b, t, d = x2.shape
    nt = t // TM
    tile = pl.BlockSpec((1, TM, d), lambda bb, i: (bb, i, 0))
    kv_shape = lambda w: jax.ShapeDtypeStruct((b, KV_HEADS, t, w), BF16)
    kv_spec = lambda w: pl.BlockSpec((1, KV_HEADS, TM, w), lambda bb, i: (bb, 0, i, 0))
    tab = pl.BlockSpec((TM, 128), lambda bb, i: (i, 0))
    weights = [gain, wqkv, qn, kn]
    segs = [consts["seg"], consts["segt"], consts["segk"], consts["segkt"]]
    return pl.pallas_call(
        _qkv1_kernel,
        grid=(b, nt),
        in_specs=[tile, pl.BlockSpec((1, 1, 6, d), lambda bb, i: (bb, jnp.minimum(i, 1), 0, 0))]
        + [_const_spec(w.shape) for w in weights] + [tab, tab] + [_const_spec(w.shape) for w in segs],
        out_specs=[tile, kv_spec(HEAD), kv_spec(2 * HEAD)],
        out_shape=[jax.ShapeDtypeStruct((b, t, d), BF16), kv_shape(HEAD), kv_shape(2 * HEAD)],
        compiler_params=_cparams(("parallel", "parallel")),
        name="attn_qkv",
    )(x2, mods, *weights, cos, sin, *segs)


def _attn_kernel(q_ref, k_ref, v_ref, o_ref):
    k = k_ref[0, 0]
    v = v_ref[0, 0]
    nheads = q_ref.shape[-1] // HEAD
    scores = lambda j: lax.dot_general(q_ref[0, :, j * HEAD:(j + 1) * HEAD], k, _NT, preferred_element_type=F32)
    outs = []
    s_next = scores(0)
    for j in range(nheads):
        s = s_next
        if j + 1 < nheads:
            s_next = scores(j + 1)
        m = jnp.max(s, axis=-1, keepdims=True)
        p = jnp.exp2(s - m).astype(BF16)
        ov = jnp.dot(p, v, preferred_element_type=F32)
        outs.append(ov[:, :HEAD] / ov[:, HEAD:])
    o_ref[0] = jnp.concatenate(outs, axis=1).astype(BF16)


def _attention(q, k, v, ctx_tiles):
    b, t, d = q.shape
    s = t - ctx_tiles * TM
    gw = d // KV_HEADS
    return pl.pallas_call(
        _attn_kernel,
        grid=(b, KV_HEADS, s // TM),
        in_specs=[pl.BlockSpec((1, TM, gw), lambda bb, gi, i: (bb, i + ctx_tiles, gi)),
                  pl.BlockSpec((1, 1, t, HEAD), lambda bb, gi, i: (bb, gi, 0, 0)),
                  pl.BlockSpec((1, 1, t, 2 * HEAD), lambda bb, gi, i: (bb, gi, 0, 0))],
        out_specs=pl.BlockSpec((1, TM, gw), lambda bb, gi, i: (bb, i, gi)),
        out_shape=jax.ShapeDtypeStruct((b, s, d), BF16),
        compiler_params=_cparams(("parallel", "parallel", "parallel")),
        name="gqa_attention",
    )(q, k, v)


def _out1_kernel(x_ref, a_ref, mod_ref, wo_ref, gain_ref, w1_ref, w2_ref, fin_ref, o_ref):
    o = jnp.dot(a_ref[0], wo_ref[...], preferred_element_type=F32)
    x1 = x_ref[0] + mod_ref[0, 0, 2:3, :] * o
    h2 = _norm_mod(x1, gain_ref[...], mod_ref[0, 0, 3:4, :], mod_ref[0, 0, 4:5, :])
    x2 = x1 + mod_ref[0, 0, 5:6, :] * _mlp(h2, w1_ref, w2_ref)
    ms = jnp.mean(x2 * x2, axis=-1, keepdims=True)
    o_ref[0] = x2 * lax.rsqrt(ms + NORM_EPS) * fin_ref[...]


def _out1(x2, att, mods, wo, gain_mlp, w1, w2, fin, ctx_tiles):
    b, s, d = att.shape
    weights = [wo, gain_mlp, w1, w2, fin]
    return pl.pallas_call(
        _out1_kernel,
        grid=(b, s // TM),
        in_specs=[pl.BlockSpec((1, TM, d), lambda bb, i: (bb, i + ctx_tiles, 0)),
                  pl.BlockSpec((1, TM, d), lambda bb, i: (bb, i, 0)),
                  pl.BlockSpec((1, 1, 6, d), lambda bb, i: (bb, 1, 0, 0))]
        + [_const_spec(w.shape) for w in weights],
        out_specs=pl.BlockSpec((1, TM, d), lambda bb, i: (bb, i, 0)),
        out_shape=jax.ShapeDtypeStruct((b, s, d), F32),
        compiler_params=_cparams(("parallel", "parallel")),
        name="attn_out_mlp_final",
    )(x2, att, mods, *weights)


def _block_diag2(a, b):
    za = jnp.zeros((a.shape[0], b.shape[1]), a.dtype)
    zb = jnp.zeros((b.shape[0], a.shape[1]), b.dtype)
    return jnp.concatenate([jnp.concatenate([a, za], 1), jnp.concatenate([zb, b], 1)], 0)


def _constants(d):
    lane = jnp.arange(d)[:, None] // HEAD
    seg = (lane == jnp.arange(SEG_W)[None, :]).astype(BF16)
    nk = KV_HEADS * HEAD
    segk = seg[:nk]
    t = jnp.arange(TM)
    same = (t[:, None] // CHUNK) == (t[None, :] // CHUNK)
    tri_f = (same & (t[None, :] <= t[:, None])).astype(BF16)
    tri_b = (same & (t[None, :] >= t[:, None])).astype(BF16)
    return dict(seg=seg, segt=seg.T, segk=segk, segkt=segk.T, tri_f=tri_f, tri_b=tri_b)


def _rope_tables(s, c_len):
    tok = jnp.arange(s)
    row = (tok // GRID_W).astype(F32)
    col = (tok % GRID_W).astype(F32)
    half = HEAD // 2
    freqs = ROPE_THETA ** (-jnp.arange(0, half, 2, dtype=F32) / half)
    ang_r = row[:, None] * freqs
    ang_c = col[:, None] * freqs
    cos = jnp.concatenate([jnp.cos(ang_r)] * 2 + [jnp.cos(ang_c)] * 2, axis=1)
    sin = jnp.concatenate([-jnp.sin(ang_r), jnp.sin(ang_r), -jnp.sin(ang_c), jnp.sin(ang_c)], axis=1)
    cos = jnp.concatenate([jnp.ones((c_len, HEAD), F32), cos], axis=0)
    sin = jnp.concatenate([jnp.zeros((c_len, HEAD), F32), sin], axis=0)
    return jnp.tile(cos, (1, 2)), jnp.tile(sin, (1, 2))


def kernel(x, c, ctx, c_ctx, w_mod, b_mod, norm_mix, norm_mlp, mlp_w1, mlp_w2, rwkv_mu, rwkv_wr, rwkv_wk, rwkv_wv,
           rwkv_wo, rwkv_w0, rwkv_w1, rwkv_w2, rwkv_a0, rwkv_a1, rwkv_a2, rwkv_g1, rwkv_g2, rwkv_k_k, rwkv_k_a,
           rwkv_r_k, rwkv_ln_w, rwkv_ln_b, attn_wqkv, attn_q_norm, attn_k_norm, attn_wo, final_norm):
    b, s, d = x.shape
    c_len = ctx.shape[1]
    assert d % (4 * 128) == 0 and d // HEAD <= SEG_W
    assert s % TM == 0 and c_len % TM == 0 and TM % GRID_W == 0 and TM % CHUNK == 0
    assert w_mod.shape[0] == 2 and rwkv_mu.shape[0] == 1 and attn_wqkv.shape[0] == 1
    ctx_tiles = c_len // TM
    row = lambda a: a.reshape(1, -1).astype(F32)
    bf = lambda a: a.astype(BF16)
    consts = _constants(d)

    rows = -(-(b + 1) // 8) * 8
    cvec = jnp.zeros((rows, d), F32).at[:b].set(c).at[b].set(c_ctx)
    m = _modulation(cvec, w_mod, b_mod)
    m = m.reshape(2, rows, 6, d)
    mods = [jnp.stack([jnp.broadcast_to(m[l, b][None], (b, 6, d)), m[l, :b]], axis=1) for l in range(2)]

    xall = jnp.concatenate([ctx, x], axis=1)

    p0 = dict(
        mu=rwkv_mu[0], wr=bf(rwkv_wr[0]), wk=bf(rwkv_wk[0]), wv=bf(rwkv_wv[0]), wo=bf(rwkv_wo[0]),
        g1=bf(rwkv_g1[0]), g2=bf(rwkv_g2[0]),
        w1=bf(jnp.concatenate([rwkv_w1[0, 0], rwkv_w1[0, 1]], axis=1)),
        w2=bf(_block_diag2(rwkv_w2[0, 0], rwkv_w2[0, 1])),
        w0=rwkv_w0[0].reshape(1, -1),
        a1=bf(jnp.concatenate([rwkv_a1[0, 0], rwkv_a1[0, 1]], axis=1)),
        a2=bf(_block_diag2(rwkv_a2[0, 0], rwkv_a2[0, 1])),
        a0=rwkv_a0[0].reshape(1, -1),
        k_k=row(rwkv_k_k[0]), k_a=row(rwkv_k_a[0]), r_k=row(rwkv_r_k[0]),
        ln_w=row(rwkv_ln_w[0]), ln_b=row(rwkv_ln_b[0]),
    )
    (v, g, bonus, a_f, r_f, b_f, k_f, a_b, r_b, b_b, k_b, dl_f, dl_b) = _proj0(
        xall, mods[0], row(norm_mix[0]), p0, consts, ctx_tiles)
    w_f, u_f = _wkv_intra(False, a_f, b_f, k_f, v)
    w_b, u_b = _wkv_intra(True, a_b, b_b, k_b, v)
    y_f, y_b = _wkv_scan(ctx_tiles, (w_f, u_f, r_f, b_f, k_f, v, dl_f), (w_b, u_b, r_b, b_b, k_b, v, dl_b))
    x1 = _readout0(xall, y_f, y_b, bonus, g, mods[0], p0, row(norm_mlp[0]), bf(mlp_w1[0]), bf(mlp_w2[0]), consts)

    cos, sin = _rope_tables(s, c_len)
    qn = jnp.tile(attn_q_norm[0], d // HEAD).reshape(1, -1)
    kn = jnp.tile(attn_k_norm[0], KV_HEADS).reshape(1, -1)
    q, k, vv = _qkv1(x1, mods[1], row(norm_mix[1]), bf(attn_wqkv[0]), qn, kn, cos, sin, consts)
    att = _attention(q, k, vv, ctx_tiles)
    return _out1(x1, att, mods[1], bf(attn_wo[0]), row(norm_mlp[1]), bf(mlp_w1[1]), bf(mlp_w2[1]),
                 row(final_norm), ctx_tiles)
```

```python
import functools
import math

import jax
import jax.numpy as jnp
from jax import lax
from jax.experimental import pallas as pl
from jax.experimental.pallas import tpu as pltpu

F32 = jnp.float32
BF16 = jnp.bfloat16

HEAD = 64
GRID_W = 64
KV_HEADS = 4
NORM_EPS = 1e-6
GN_EPS = 64e-5
ROPE_THETA = 10000.0
CHUNK = 64
TM = 256
SEG_W = 128
VMEM_LIMIT = 56 * 1024 * 1024
EXP_NEG_HALF = math.exp(-0.5)
LOG2E = math.log2(math.e)
KEY_BLOCK = 256


def _bdot(a, b):
    return jnp.dot(a.astype(BF16), b.astype(BF16), preferred_element_type=F32)


def _split_hi_lo(x):
    hi = x.astype(BF16)
    lo = (x - hi.astype(F32)).astype(BF16)
    return hi, lo


def _seg_sum(x, seg, segt, split_input=True):
    if split_input:
        hi, lo = _split_hi_lo(x)
        s = jnp.dot(hi, seg, preferred_element_type=F32) + jnp.dot(lo, seg, preferred_element_type=F32)
    else:
        s = jnp.dot(x.astype(BF16), seg, preferred_element_type=F32)
    s_hi, s_lo = _split_hi_lo(s)
    return jnp.dot(s_hi, segt, preferred_element_type=F32) + jnp.dot(s_lo, segt, preferred_element_type=F32)


def _sigmoid(x):
    return 0.5 * jnp.tanh(0.5 * x) + 0.5


def _norm_mod(x, gain, shift, scale):
    ms = jnp.mean(x * x, axis=-1, keepdims=True)
    return x * lax.rsqrt(ms + NORM_EPS) * (gain * (1.0 + scale)) + shift


def _cparams(sem):
    return pltpu.CompilerParams(dimension_semantics=sem, vmem_limit_bytes=VMEM_LIMIT)


def _const_spec(shape):
    nd = len(shape)
    return pl.BlockSpec(shape, lambda *_: (0,) * nd)


def _mod_kernel(c_ref, w_ref, b_ref, o_ref):
    c = c_ref[...]
    s = c * jax.nn.sigmoid(c)
    s_hi, s_lo = _split_hi_lo(s)
    w = w_ref[0]
    w_hi, w_lo = _split_hi_lo(w)
    acc = jnp.dot(s_hi, w_hi, preferred_element_type=F32)
    acc += jnp.dot(s_hi, w_lo, preferred_element_type=F32)
    acc += jnp.dot(s_lo, w_hi, preferred_element_type=F32)
    o_ref[0] = acc + b_ref[0]


def _modulation(cvec, w_mod, b_mod):
    depth, d, n = w_mod.shape
    rows = cvec.shape[0]
    tn = 512
    return pl.pallas_call(
        _mod_kernel,
        grid=(depth, n // tn),
        in_specs=[
            pl.BlockSpec((rows, d), lambda l, j: (0, 0)),
            pl.BlockSpec((1, d, tn), lambda l, j: (l, 0, j)),
            pl.BlockSpec((1, 1, tn), lambda l, j: (l, 0, j)),
        ],
        out_specs=pl.BlockSpec((1, rows, tn), lambda l, j: (l, 0, j)),
        out_shape=jax.ShapeDtypeStruct((depth, rows, n), F32),
        compiler_params=_cparams(("parallel", "parallel")),
        name="adaln_modulation",
    )(cvec, w_mod, b_mod.reshape(depth, 1, n))


def _proj0_kernel(nt, xc_ref, xp_ref, xn_ref, mod_ref, gain_ref, mu_ref, wr_ref, wk_ref, wv_ref, g1_ref, g2_ref,
                  w1_ref, w2_ref, w0_ref, a1_ref, a2_ref, a0_ref, kk_ref, ka_ref, rk_ref, seg_ref, segt_ref,
                  trif_ref, trib_ref,
                  v_out, g_out, bonus_out, af_out, rf_out, bf_out, kf_out, ab_out, rb_out, bb_out, kb_out,
                  dlf_out, dlb_out, hs_ref):
    i = pl.program_id(1)
    d = xc_ref.shape[-1]
    q = d // 4
    gain = gain_ref[...]
    shift = mod_ref[0, 0, 0:1, :]
    scale = mod_ref[0, 0, 1:2, :]
    h = _norm_mod(xc_ref[0], gain, shift, scale)
    t_idx = lax.broadcasted_iota(jnp.int32, (TM, 1), 0)

    @pl.when(i == 0)
    def _():
        prev = jnp.where(t_idx == 0, 0.0, pltpu.roll(h[:, :2 * q], 1, 0))
        nxt = jnp.where(t_idx == TM - 1, 0.0, pltpu.roll(h[:, 2 * q:], TM - 1, 0))
        hs_ref[:, :2 * q] = prev
        hs_ref[:, 2 * q:] = nxt

    @pl.when(i > 0)
    def _():
        col = t_idx % GRID_W
        left = jnp.where(col == 0, 0.0, pltpu.roll(h[:, :q], 1, 0))
        right = jnp.where(col == GRID_W - 1, 0.0, pltpu.roll(h[:, q:2 * q], TM - 1, 0))
        hp = _norm_mod(xp_ref[0], gain, shift, scale)[:, 2 * q:3 * q]
        hn = _norm_mod(xn_ref[0], gain, shift, scale)[:, 3 * q:]
        hp = jnp.where(i > 1, hp, 0.0)
        hn = jnp.where(i < nt - 1, hn, 0.0)
        hs_ref[:, :q] = left
        hs_ref[:, q:2 * q] = right
        hs_ref[:, 2 * q:3 * q] = jnp.concatenate([hp, h[:TM - GRID_W, 2 * q:3 * q]], axis=0)
        hs_ref[:, 3 * q:] = jnp.concatenate([h[GRID_W:, 3 * q:], hn], axis=0)

    hb = h.astype(BF16)
    xxb = (hs_ref[...] - h).astype(BF16)
    mub = mu_ref[...].astype(BF16)
    mix = lambda j: hb + xxb * mub[j:j + 1, :]
    seg = seg_ref[...]
    segt = segt_ref[...]

    r = jnp.dot(mix(0), wr_ref[...], preferred_element_type=F32)
    k = jnp.dot(mix(2), wk_ref[...], preferred_element_type=F32)
    v = jnp.dot(mix(3), wv_ref[...], preferred_element_type=F32)
    g = _bdot(_sigmoid(jnp.dot(mix(5), g1_ref[...], preferred_element_type=F32)), g2_ref[...])
    wl = _bdot(jnp.tanh(jnp.dot(mix(1), w1_ref[...], preferred_element_type=F32)), w2_ref[...]) + w0_ref[...]
    al = _bdot(jnp.dot(mix(4), a1_ref[...], preferred_element_type=F32), a2_ref[...]) + a0_ref[...]
    v_out[0] = v.astype(BF16)
    g_out[0] = g.astype(BF16)

    kk = k * kk_ref[...]
    kk = kk * lax.rsqrt(_seg_sum(kk * kk, seg, segt, split_input=False) + 1e-12)

    outs = ((af_out, rf_out, bf_out, kf_out, dlf_out, trif_ref), (ab_out, rb_out, bb_out, kb_out, dlb_out, trib_ref))
    ksum = None
    for dr, (a_o, r_o, b_o, k_o, dl_o, tri_ref) in enumerate(outs):
        e = EXP_NEG_HALF * _sigmoid(wl[:, dr * d:(dr + 1) * d])
        a_lr = _sigmoid(al[:, dr * d:(dr + 1) * d])
        k_dir = k * (1.0 + (a_lr - 1.0) * ka_ref[...])
        ksum = k_dir if ksum is None else ksum + k_dir
        e_hi, e_lo = _split_hi_lo(e)
        tri = tri_ref[...]
        c = jnp.dot(tri, e_hi, preferred_element_type=F32) + jnp.dot(tri, e_lo, preferred_element_type=F32)
        grow = jnp.exp(c)
        shrink = jnp.exp(-c)
        a_o[0] = (-kk * jnp.exp(e - c)).astype(BF16)
        r_o[0] = (r * shrink).astype(BF16)
        b_o[0] = (kk * a_lr * grow).astype(BF16)
        k_o[0] = (k_dir * grow).astype(BF16)
        last = CHUNK - 1 if dr == 0 else 0
        dl_o[0, 0] = jnp.concatenate(
            [shrink[n * CHUNK + last:n * CHUNK + last + 1, :] for n in range(TM // CHUNK)], axis=0)

    bonus_out[0] = (_seg_sum(r * ksum * rk_ref[...], seg, segt) * v).astype(BF16)


def _proj0(xall, mods, gain, p, consts, ctx_tiles):
    b, t, d = xall.shape
    nt = t // TM
    hb = TM // GRID_W
    nhb = t // GRID_W
    tok = lambda bb, i: (bb, i, 0)
    big = jax.ShapeDtypeStruct((b, t, d), BF16)
    dl = jax.ShapeDtypeStruct((b, nt, TM // CHUNK, d), F32)
    tile_spec = pl.BlockSpec((1, TM, d), tok)
    dl_spec = pl.BlockSpec((1, 1, TM // CHUNK, d), lambda bb, i: (bb, i, 0, 0))
    weights = [p["mu"], p["wr"], p["wk"], p["wv"], p["g1"], p["g2"], p["w1"], p["w2"], p["w0"],
               p["a1"], p["a2"], p["a0"], p["k_k"], p["k_a"], p["r_k"],
               consts["seg"], consts["segt"], consts["tri_f"], consts["tri_b"]]
    return pl.pallas_call(
        functools.partial(_proj0_kernel, nt),
        grid=(b, nt),
        in_specs=[
            tile_spec,
            pl.BlockSpec((1, GRID_W, d), lambda bb, i: (bb, jnp.maximum(i * hb - 1, 0), 0)),
            pl.BlockSpec((1, GRID_W, d), lambda bb, i: (bb, jnp.minimum((i + 1) * hb, nhb - 1), 0)),
            pl.BlockSpec((1, 1, 6, d), lambda bb, i: (bb, jnp.minimum(i, 1), 0, 0)),
            _const_spec(gain.shape),
        ] + [_const_spec(w.shape) for w in weights],
        out_specs=[tile_spec] * 11 + [dl_spec] * 2,
        out_shape=[big] * 11 + [dl] * 2,
        scratch_shapes=[pltpu.VMEM((TM, d), F32)],
        compiler_params=_cparams(("parallel", "parallel")),
        name="rwkv_project",
    )(xall, xall, xall, mods, gain, *weights)


_NT = (((1,), (1,)), ((), ()))
_TN = (((0,), (0,)), ((), ()))
PAIR = 2 * HEAD


def _pair_masks(rows):
    lane = lax.broadcasted_iota(jnp.int32, (rows, PAIR), 1)
    return lane < HEAD, lane >= HEAD


def _tri_mask(reverse, inclusive):
    row = lax.broadcasted_iota(jnp.int32, (CHUNK, PAIR), 0)
    col = lax.broadcasted_iota(jnp.int32, (CHUNK, PAIR), 1) % HEAD
    if inclusive:
        return (col >= row) if reverse else (col <= row)
    return (col > row) if reverse else (col < row)


def _wkv_intra_kernel(reverse, a_ref, b_ref, k_ref, v_ref, w_out, u_out):
    npairs = a_ref.shape[-1] // PAIR
    h0, h1 = _pair_masks(CHUNK)
    lane2 = lax.broadcasted_iota(jnp.int32, (CHUNK, 2 * PAIR), 1)
    row2 = lax.broadcasted_iota(jnp.int32, (CHUNK, 2 * PAIR), 0)
    first = (lane2 // HEAD) % 2 == 0
    left = lane2 < PAIR
    col2 = lane2 % HEAD
    strict = (col2 > row2) if reverse else (col2 < row2)
    eye_right = ((col2 == row2) & (lane2 >= PAIR)).astype(F32)
    stack2 = lambda t: jnp.concatenate([jnp.where(h0, t, 0), jnp.where(h1, t, 0)], axis=0)

    per_iter = 2

    def chunk(ci, carry):
        rows = [pl.ds(pl.multiple_of((ci * per_iter + n) * CHUNK, CHUNK), CHUNK) for n in range(per_iter)]
        psl = lambda p: slice(p * PAIR, (p + 1) * PAIR)
        slabs = [(n, p) for n in range(per_iter) for p in range(npairs)]
        a = [a_ref[0, rows[n], psl(p)] for n, p in slabs]
        v = [v_ref[0, rows[n], psl(p)] for n, p in slabs]
        bk = [jnp.concatenate([stack2(b_ref[0, rows[n], psl(p)]), stack2(k_ref[0, rows[n], psl(p)])], axis=0)
              for n, p in slabs]
        sc = [jnp.where(strict, lax.dot_general(a[g], bk[g], _NT, preferred_element_type=F32), 0.0)
              for g in range(len(slabs))]
        x = [jnp.where(left, t, eye_right) for t in sc]
        for _ in range(6):
            xb = [t.astype(BF16) for t in x]
            wts = [jnp.concatenate([jnp.where(first, t, 0), jnp.where(first, 0, t)], axis=0) for t in xb]
            res = [jnp.dot(t[:, :PAIR], w, preferred_element_type=F32) for t, w in zip(xb, wts)]
            x = [r + jnp.where(left, 0.0, t) for r, t in zip(res, x)]
        tinv = [t[:, PAIR:].astype(BF16) for t in x]
        akv = [jnp.dot(sc[g][:, PAIR:].astype(BF16), stack2(v[g]), preferred_element_type=F32).astype(BF16)
               for g in range(len(slabs))]
        wu = [jnp.dot(tinv[g], jnp.concatenate([stack2(a[g]), stack2(akv[g])], axis=1),
                      preferred_element_type=F32) for g in range(len(slabs))]
        for n in range(per_iter):
            mine = wu[n * npairs:(n + 1) * npairs]
            w_out[0, rows[n], :] = jnp.concatenate([t[:, :PAIR] for t in mine], axis=1).astype(BF16)
            u_out[0, rows[n], :] = jnp.concatenate([t[:, PAIR:] for t in mine], axis=1)
        return carry

    lax.fori_loop(0, a_ref.shape[1] // (CHUNK * per_iter), chunk, 0)


def _wkv_intra(reverse, a, bm, k, v):
    b, t, d = a.shape
    tile = pl.BlockSpec((1, TM, d), lambda bb, i: (bb, i, 0))
    return pl.pallas_call(
        functools.partial(_wkv_intra_kernel, reverse),
        grid=(b, t // TM),
        in_specs=[tile] * 4,
        out_specs=[tile, tile],
        out_shape=[jax.ShapeDtypeStruct((b, t, d), BF16), jax.ShapeDtypeStruct((b, t, d), F32)],
        compiler_params=_cparams(("parallel", "parallel")),
        name="wkv_intra_bwd" if reverse else "wkv_intra_fwd",
    )(a, bm, k, v)


def _wkv_scan_kernel(wf, uf, rf, bf, kf, vf, dlf, wb, ub, rb, bb, kb, vb, dlb, yf_ref, yb_ref, z_ref):
    dirs = ((False, wf, uf, rf, bf, kf, vf, dlf, yf_ref), (True, wb, ub, rb, bb, kb, vb, dlb, yb_ref))
    npairs = wf.shape[-1] // PAIR
    nchunks = wf.shape[1] // CHUNK
    slabs = [(dd, p) for dd in range(2) for p in range(npairs)]
    h0, h1 = _pair_masks(CHUNK)
    stack2 = lambda t: jnp.concatenate([jnp.where(h0, t, 0), jnp.where(h1, t, 0)], axis=0)
    lane2 = lax.broadcasted_iota(jnp.int32, (CHUNK, 2 * PAIR), 1) % HEAD
    row2 = lax.broadcasted_iota(jnp.int32, (CHUNK, 2 * PAIR), 0)
    incl = (lane2 <= row2, lane2 >= row2)
    brow = lax.broadcasted_iota(jnp.int32, (PAIR, PAIR), 0) // HEAD
    bcol = lax.broadcasted_iota(jnp.int32, (PAIR, PAIR), 1) // HEAD
    same_head = brow == bcol
    psl = lambda p: slice(p * PAIR, (p + 1) * PAIR)

    @pl.when(pl.program_id(1) == 0)
    def _():
        z_ref[...] = jnp.zeros(z_ref.shape, F32)

    z = [z_ref[dd, p] for dd, p in slabs]
    for n in range(nchunks):
        rows = [slice((nchunks - 1 - n) * CHUNK, (nchunks - n) * CHUNK) if dirs[dd][0]
                else slice(n * CHUNK, (n + 1) * CHUNK) for dd, _ in slabs]
        cidx = [(nchunks - 1 - n) if dirs[dd][0] else n for dd, _ in slabs]
        ld = lambda pos, g: dirs[slabs[g][0]][pos][0, rows[g], psl(slabs[g][1])]
        ng = len(slabs)
        r = [ld(3, g) for g in range(ng)]
        v = [ld(6, g) for g in range(ng)]
        bm = [ld(4, g) for g in range(ng)]
        km = [ld(5, g) for g in range(ng)]
        bk = [jnp.concatenate([bm[g], km[g]], axis=0) for g in range(ng)]
        bk4 = [jnp.concatenate([stack2(bm[g]), stack2(km[g])], axis=0) for g in range(ng)]
        wr = [jnp.concatenate([ld(1, g), r[g]], axis=0) for g in range(ng)]
        sr = [jnp.where(incl[slabs[g][0]], lax.dot_general(r[g], bk4[g], _NT, preferred_element_type=F32),
                        0.0).astype(BF16) for g in range(ng)]
        ws = [lax.dot_general(wr[g], z[g].astype(BF16), _NT, preferred_element_type=F32) for g in range(ng)]
        ub = [(ws[g][:CHUNK] + ld(2, g)).astype(BF16) for g in range(ng)]
        uv = [jnp.concatenate([ub[g], v[g]], axis=0) for g in range(ng)]
        uv4 = [jnp.concatenate([stack2(ub[g]), stack2(v[g])], axis=0) for g in range(ng)]
        ys = [ws[g][CHUNK:] + jnp.dot(sr[g], uv4[g], preferred_element_type=F32) for g in range(ng)]
        upd = [lax.dot_general(uv[g], bk[g], _TN, preferred_element_type=F32) for g in range(ng)]
        z = [(z[g] + jnp.where(same_head, upd[g], 0.0))
             * dirs[slabs[g][0]][7][0, 0, cidx[g]:cidx[g] + 1, psl(slabs[g][1])] for g in range(ng)]
        for dd in range(2):
            dirs[dd][8][0, rows[dd * npairs], :] = jnp.concatenate(ys[dd * npairs:(dd + 1) * npairs], axis=1)
    for g, (dd, p) in enumerate(slabs):
        z_ref[dd, p] = z[g]


def _wkv_scan(ctx_tiles, fwd, bwd):
    b, t, d = fwd[0].shape
    nt = t // TM
    blk_b = lambda j: jnp.where(j < ctx_tiles, ctx_tiles - 1 - j, nt - 1 - j + ctx_tiles)
    specs = []
    for blk in (lambda j: j, blk_b):
        tile = pl.BlockSpec((1, TM, d), lambda bb, j, blk=blk: (bb, blk(j), 0))
        dl_spec = pl.BlockSpec((1, 1, TM // CHUNK, d), lambda bb, j, blk=blk: (bb, blk(j), 0, 0))
        specs.append((tile, dl_spec))
    return pl.pallas_call(
        _wkv_scan_kernel,
        grid=(b, nt),
        in_specs=[specs[0][0]] * 6 + [specs[0][1]] + [specs[1][0]] * 6 + [specs[1][1]],
        out_specs=[specs[0][0], specs[1][0]],
        out_shape=[jax.ShapeDtypeStruct((b, t, d), F32)] * 2,
        scratch_shapes=[pltpu.VMEM((2, d // PAIR, PAIR, PAIR), F32)],
        compiler_params=_cparams(("parallel", "arbitrary")),
        name="wkv_scan",
    )(*fwd, *bwd)


def _mlp(h, w1_ref, w2_ref):
    hb = h.astype(BF16)
    dff = w1_ref.shape[1]
    fc = 1024
    acc = None
    for j in range(dff // fc):
        a = jnp.dot(hb, w1_ref[:, j * fc:(j + 1) * fc], preferred_element_type=F32)
        a = jnp.square(jnp.maximum(a, 0.0)).astype(BF16)
        part = jnp.dot(a, w2_ref[j * fc:(j + 1) * fc, :], preferred_element_type=F32)
        acc = part if acc is None else acc + part
    return acc


def _readout0_kernel(x_ref, yf_ref, yb_ref, bonus_ref, g_ref, mod_ref, lnw_ref, lnb_ref, wo_ref, gain_ref,
                     w1_ref, w2_ref, seg_ref, segt_ref, o_ref):
    seg = seg_ref[...]
    segt = segt_ref[...]
    y = yf_ref[0] + yb_ref[0]
    mean = _seg_sum(y, seg, segt) * (1.0 / HEAD)
    dev = y - mean
    var = _seg_sum(dev * dev, seg, segt) * (1.0 / HEAD)
    yn = dev * lax.rsqrt(var + GN_EPS) * lnw_ref[...] + lnb_ref[...]
    mixed = (yn + bonus_ref[0].astype(F32)) * g_ref[0].astype(F32)
    o = _bdot(mixed, wo_ref[...])
    x1 = x_ref[0] + mod_ref[0, 0, 2:3, :] * o
    h2 = _norm_mod(x1, gain_ref[...], mod_ref[0, 0, 3:4, :], mod_ref[0, 0, 4:5, :])
    o_ref[0] = x1 + mod_ref[0, 0, 5:6, :] * _mlp(h2, w1_ref, w2_ref)


def _readout0(xall, yf, yb, bonus, g, mods, p, gain_mlp, w1, w2, consts):
    b, t, d = xall.shape
    nt = t // TM
    tile = pl.BlockSpec((1, TM, d), lambda bb, i: (bb, i, 0))
    weights = [p["ln_w"], p["ln_b"], p["wo"], gain_mlp, w1, w2, consts["seg"], consts["segt"]]
    return pl.pallas_call(
        _readout0_kernel,
        grid=(b, nt),
        in_specs=[tile] * 5 + [pl.BlockSpec((1, 1, 6, d), lambda bb, i: (bb, jnp.minimum(i, 1), 0, 0))]
        + [_const_spec(w.shape) for w in weights],
        out_specs=tile,
        out_shape=jax.ShapeDtypeStruct((b, t, d), F32),
        compiler_params=_cparams(("parallel", "parallel")),
        name="rwkv_readout_mlp",
    )(xall, yf, yb, bonus, g, mods, *weights)


def _rope(x, cos, sin_signed):
    lane = lax.broadcasted_iota(jnp.int32, (1, 128), 1)
    low = (lane & 16) == 0
    cols = []
    for j in range(x.shape[1] // 128):
        xj = x[:, j * 128:(j + 1) * 128]
        partner = jnp.where(low, pltpu.roll(xj, 128 - 16, 1), pltpu.roll(xj, 16, 1))
        cols.append(xj * cos + partner * sin_signed)
    return jnp.concatenate(cols, axis=1)


def _qkv1_kernel(x_ref, mod_ref, gain_ref, w_ref, wvt_ref, qn_ref, kn_ref, cos_ref, sin_ref, seg_ref, segt_ref,
                 segk_ref, segkt_ref, q_out, k_out, vt_out):
    d = x_ref.shape[-1]
    nk = KV_HEADS * HEAD
    hb = _norm_mod(x_ref[0], gain_ref[...], mod_ref[0, 0, 0:1, :], mod_ref[0, 0, 1:2, :]).astype(BF16)
    qkv = jnp.dot(hb, w_ref[...], preferred_element_type=F32)
    vt = lax.dot_general(wvt_ref[...], hb, _NT, preferred_element_type=F32).astype(BF16)
    cos = cos_ref[...]
    sin = sin_ref[...]
    q = qkv[:, :d]
    q = q * lax.rsqrt(_seg_sum(q * q, seg_ref[...], segt_ref[...]) * (1.0 / HEAD) + NORM_EPS) * qn_ref[...]
    q_out[0] = (_rope(q, cos, sin) * (LOG2E * HEAD ** -0.5)).astype(BF16)
    k = qkv[:, d:d + nk]
    k = k * lax.rsqrt(_seg_sum(k * k, segk_ref[...], segkt_ref[...]) * (1.0 / HEAD) + NORM_EPS) * kn_ref[...]
    k = _rope(k, cos, sin).astype(BF16)
    ones = jnp.ones((HEAD, vt.shape[1]), BF16)
    for gi in range(KV_HEADS):
        k_out[0, gi] = k[:, gi * HEAD:(gi + 1) * HEAD]
        vt_out[0, gi] = jnp.concatenate([vt[gi * HEAD:(gi + 1) * HEAD, :], ones], axis=0)


def _qkv1(x2, mods, gain, wqk, wvt, qn, kn, cos, sin, consts):
    b, t, d = x2.shape
    nt = t // TM
    tile = pl.BlockSpec((1, TM, d), lambda bb, i: (bb, i, 0))
    tab = pl.BlockSpec((TM, 128), lambda bb, i: (i, 0))
    weights = [gain, wqk, wvt, qn, kn]
    segs = [consts["seg"], consts["segt"], consts["segk"], consts["segkt"]]
    return pl.pallas_call(
        _qkv1_kernel,
        grid=(b, nt),
        in_specs=[tile, pl.BlockSpec((1, 1, 6, d), lambda bb, i: (bb, jnp.minimum(i, 1), 0, 0))]
        + [_const_spec(w.shape) for w in weights] + [tab, tab] + [_const_spec(w.shape) for w in segs],
        out_specs=[tile,
                   pl.BlockSpec((1, KV_HEADS, TM, HEAD), lambda bb, i: (bb, 0, i, 0)),
                   pl.BlockSpec((1, KV_HEADS, 2 * HEAD, TM), lambda bb, i: (bb, 0, 0, i))],
        out_shape=[jax.ShapeDtypeStruct((b, t, d), BF16),
                   jax.ShapeDtypeStruct((b, KV_HEADS, t, HEAD), BF16),
                   jax.ShapeDtypeStruct((b, KV_HEADS, 2 * HEAD, t), BF16)],
        compiler_params=_cparams(("parallel", "parallel")),
        name="attn_qkv",
    )(x2, mods, *weights, cos, sin, *segs)


def _attn_kernel(q_ref, k_ref, vt_ref, o_ref):
    nheads = q_ref.shape[-1] // HEAD
    nblk = k_ref.shape[2] // KEY_BLOCK
    q = [q_ref[0, :, j * HEAD:(j + 1) * HEAD] for j in range(nheads)]

    def scores(i):
        kb = k_ref[0, 0, i * KEY_BLOCK:(i + 1) * KEY_BLOCK, :]
        return [lax.dot_general(kb, qj, _NT, preferred_element_type=F32) for qj in q]

    m = [None] * nheads
    acc = [None] * nheads
    s_next = scores(0)
    for i in range(nblk):
        s = s_next
        if i + 1 < nblk:
            s_next = scores(i + 1)
        vb = vt_ref[0, 0, :, i * KEY_BLOCK:(i + 1) * KEY_BLOCK]
        for j in range(nheads):
            bmax = jnp.max(s[j], axis=0, keepdims=True)
            m_new = bmax if i == 0 else jnp.maximum(m[j], bmax)
            p = jnp.exp2(s[j] - m_new).astype(BF16)
            pv = jnp.dot(vb, p, preferred_element_type=F32)
            acc[j] = pv if i == 0 else acc[j] * jnp.exp2(m[j] - m_new) + pv
            m[j] = m_new
    outs = [a[:HEAD] / a[HEAD:HEAD + 1] for a in acc]
    o_ref[0] = jnp.concatenate(outs, axis=0).T.astype(BF16)


def _attention(q, k, v, ctx_tiles):
    b, t, d = q.shape
    s = t - ctx_tiles * TM
    gw = d // KV_HEADS
    return pl.pallas_call(
        _attn_kernel,
        grid=(b, KV_HEADS, s // TM),
        in_specs=[pl.BlockSpec((1, TM, gw), lambda bb, gi, i: (bb, i + ctx_tiles, gi)),
                  pl.BlockSpec((1, 1, t, HEAD), lambda bb, gi, i: (bb, gi, 0, 0)),
                  pl.BlockSpec((1, 1, 2 * HEAD, t), lambda bb, gi, i: (bb, gi, 0, 0))],
        out_specs=pl.BlockSpec((1, TM, gw), lambda bb, gi, i: (bb, i, gi)),
        out_shape=jax.ShapeDtypeStruct((b, s, d), BF16),
        compiler_params=_cparams(("parallel", "parallel", "parallel")),
        name="gqa_attention",
    )(q, k, v)


def _out1_kernel(x_ref, a_ref, mod_ref, wo_ref, gain_ref, w1_ref, w2_ref, fin_ref, o_ref):
    o = jnp.dot(a_ref[0], wo_ref[...], preferred_element_type=F32)
    x1 = x_ref[0] + mod_ref[0, 0, 2:3, :] * o
    h2 = _norm_mod(x1, gain_ref[...], mod_ref[0, 0, 3:4, :], mod_ref[0, 0, 4:5, :])
    x2 = x1 + mod_ref[0, 0, 5:6, :] * _mlp(h2, w1_ref, w2_ref)
    ms = jnp.mean(x2 * x2, axis=-1, keepdims=True)
    o_ref[0] = x2 * lax.rsqrt(ms + NORM_EPS) * fin_ref[...]


def _out1(x2, att, mods, wo, gain_mlp, w1, w2, fin, ctx_tiles):
    b, s, d = att.shape
    weights = [wo, gain_mlp, w1, w2, fin]
    return pl.pallas_call(
        _out1_kernel,
        grid=(b, s // TM),
        in_specs=[pl.BlockSpec((1, TM, d), lambda bb, i: (bb, i + ctx_tiles, 0)),
                  pl.BlockSpec((1, TM, d), lambda bb, i: (bb, i, 0)),
                  pl.BlockSpec((1, 1, 6, d), lambda bb, i: (bb, 1, 0, 0))]
        + [_const_spec(w.shape) for w in weights],
        out_specs=pl.BlockSpec((1, TM, d), lambda bb, i: (bb, i, 0)),
        out_shape=jax.ShapeDtypeStruct((b, s, d), F32),
        compiler_params=_cparams(("parallel", "parallel")),
        name="attn_out_mlp_final",
    )(x2, att, mods, *weights)


def _block_diag2(a, b):
    za = jnp.zeros((a.shape[0], b.shape[1]), a.dtype)
    zb = jnp.zeros((b.shape[0], a.shape[1]), b.dtype)
    return jnp.concatenate([jnp.concatenate([a, za], 1), jnp.concatenate([zb, b], 1)], 0)


def _constants(d):
    lane = jnp.arange(d)[:, None] // HEAD
    seg = (lane == jnp.arange(SEG_W)[None, :]).astype(BF16)
    nk = KV_HEADS * HEAD
    segk = seg[:nk]
    t = jnp.arange(TM)
    same = (t[:, None] // CHUNK) == (t[None, :] // CHUNK)
    tri_f = (same & (t[None, :] <= t[:, None])).astype(BF16)
    tri_b = (same & (t[None, :] >= t[:, None])).astype(BF16)
    return dict(seg=seg, segt=seg.T, segk=segk, segkt=segk.T, tri_f=tri_f, tri_b=tri_b)


def _rope_tables(s, c_len):
    tok = jnp.arange(s)
    row = (tok // GRID_W).astype(F32)
    col = (tok % GRID_W).astype(F32)
    half = HEAD // 2
    freqs = ROPE_THETA ** (-jnp.arange(0, half, 2, dtype=F32) / half)
    ang_r = row[:, None] * freqs
    ang_c = col[:, None] * freqs
    cos = jnp.concatenate([jnp.cos(ang_r)] * 2 + [jnp.cos(ang_c)] * 2, axis=1)
    sin = jnp.concatenate([-jnp.sin(ang_r), jnp.sin(ang_r), -jnp.sin(ang_c), jnp.sin(ang_c)], axis=1)
    cos = jnp.concatenate([jnp.ones((c_len, HEAD), F32), cos], axis=0)
    sin = jnp.concatenate([jnp.zeros((c_len, HEAD), F32), sin], axis=0)
    return jnp.tile(cos, (1, 2)), jnp.tile(sin, (1, 2))


def kernel(x, c, ctx, c_ctx, w_mod, b_mod, norm_mix, norm_mlp, mlp_w1, mlp_w2, rwkv_mu, rwkv_wr, rwkv_wk, rwkv_wv,
           rwkv_wo, rwkv_w0, rwkv_w1, rwkv_w2, rwkv_a0, rwkv_a1, rwkv_a2, rwkv_g1, rwkv_g2, rwkv_k_k, rwkv_k_a,
           rwkv_r_k, rwkv_ln_w, rwkv_ln_b, attn_wqkv, attn_q_norm, attn_k_norm, attn_wo, final_norm):
    b, s, d = x.shape
    c_len = ctx.shape[1]
    assert d % (4 * 128) == 0 and d // HEAD <= SEG_W
    assert s % TM == 0 and c_len % TM == 0 and TM % GRID_W == 0 and TM % CHUNK == 0
    assert w_mod.shape[0] == 2 and rwkv_mu.shape[0] == 1 and attn_wqkv.shape[0] == 1
    ctx_tiles = c_len // TM
    row = lambda a: a.reshape(1, -1).astype(F32)
    bf = lambda a: a.astype(BF16)
    consts = _constants(d)

    rows = -(-(b + 1) // 8) * 8
    cvec = jnp.zeros((rows, d), F32).at[:b].set(c).at[b].set(c_ctx)
    m = _modulation(cvec, w_mod, b_mod)
    m = m.reshape(2, rows, 6, d)
    mods = [jnp.stack([jnp.broadcast_to(m[l, b][None], (b, 6, d)), m[l, :b]], axis=1) for l in range(2)]

    xall = jnp.concatenate([ctx, x], axis=1)

    p0 = dict(
        mu=rwkv_mu[0], wr=bf(rwkv_wr[0]), wk=bf(rwkv_wk[0]), wv=bf(rwkv_wv[0]), wo=bf(rwkv_wo[0]),
        g1=bf(rwkv_g1[0]), g2=bf(rwkv_g2[0]),
        w1=bf(jnp.concatenate([rwkv_w1[0, 0], rwkv_w1[0, 1]], axis=1)),
        w2=bf(_block_diag2(rwkv_w2[0, 0], rwkv_w2[0, 1])),
        w0=rwkv_w0[0].reshape(1, -1),
        a1=bf(jnp.concatenate([rwkv_a1[0, 0], rwkv_a1[0, 1]], axis=1)),
        a2=bf(_block_diag2(rwkv_a2[0, 0], rwkv_a2[0, 1])),
        a0=rwkv_a0[0].reshape(1, -1),
        k_k=row(rwkv_k_k[0]), k_a=row(rwkv_k_a[0]), r_k=row(rwkv_r_k[0]),
        ln_w=row(rwkv_ln_w[0]), ln_b=row(rwkv_ln_b[0]),
    )
    (v, g, bonus, a_f, r_f, b_f, k_f, a_b, r_b, b_b, k_b, dl_f, dl_b) = _proj0(
        xall, mods[0], row(norm_mix[0]), p0, consts, ctx_tiles)
    w_f, u_f = _wkv_intra(False, a_f, b_f, k_f, v)
    w_b, u_b = _wkv_intra(True, a_b, b_b, k_b, v)
    y_f, y_b = _wkv_scan(ctx_tiles, (w_f, u_f, r_f, b_f, k_f, v, dl_f), (w_b, u_b, r_b, b_b, k_b, v, dl_b))
    x1 = _readout0(xall, y_f, y_b, bonus, g, mods[0], p0, row(norm_mlp[0]), bf(mlp_w1[0]), bf(mlp_w2[0]), consts)

    cos, sin = _rope_tables(s, c_len)
    qn = jnp.tile(attn_q_norm[0], d // HEAD).reshape(1, -1)
    kn = jnp.tile(attn_k_norm[0], KV_HEADS).reshape(1, -1)
    nqk = d + KV_HEADS * HEAD
    q, k, vv = _qkv1(x1, mods[1], row(norm_mix[1]), bf(attn_wqkv[0, :, :nqk]), bf(attn_wqkv[0, :, nqk:].T),
                     qn, kn, cos, sin, consts)
    att = _attention(q, k, vv, ctx_tiles)
    return _out1(x1, att, mods[1], bf(attn_wo[0]), row(norm_mlp[1]), bf(mlp_w1[1]), bf(mlp_w2[1]),
                 row(final_norm), ctx_tiles)
```

```python
import functools
import math

import jax
import jax.numpy as jnp
from jax import lax
from jax.experimental import pallas as pl
from jax.experimental.pallas import tpu as pltpu

F32 = jnp.float32
BF16 = jnp.bfloat16

HEAD = 64
GRID_W = 64
KV_HEADS = 4
NORM_EPS = 1e-6
GN_EPS = 64e-5
ROPE_THETA = 10000.0
CHUNK = 64
TM = 256
SEG_W = 128
VMEM_LIMIT = 56 * 1024 * 1024
EXP_NEG_HALF = math.exp(-0.5)
LOG2E = math.log2(math.e)
KEY_BLOCK = 512
TQ = 256
VT_ROWS = HEAD + 16


def _bdot(a, b):
    return jnp.dot(a.astype(BF16), b.astype(BF16), preferred_element_type=F32)


def _split_hi_lo(x):
    hi = x.astype(BF16)
    lo = (x - hi.astype(F32)).astype(BF16)
    return hi, lo


def _seg_sum(x, seg, segt, split_input=True):
    if split_input:
        hi, lo = _split_hi_lo(x)
        s = jnp.dot(hi, seg, preferred_element_type=F32) + jnp.dot(lo, seg, preferred_element_type=F32)
    else:
        s = jnp.dot(x.astype(BF16), seg, preferred_element_type=F32)
    s_hi, s_lo = _split_hi_lo(s)
    return jnp.dot(s_hi, segt, preferred_element_type=F32) + jnp.dot(s_lo, segt, preferred_element_type=F32)


def _sigmoid(x):
    return 0.5 * jnp.tanh(0.5 * x) + 0.5


def _norm_mod(x, gain, shift, scale):
    ms = jnp.mean(x * x, axis=-1, keepdims=True)
    return x * lax.rsqrt(ms + NORM_EPS) * (gain * (1.0 + scale)) + shift


def _cparams(sem):
    return pltpu.CompilerParams(dimension_semantics=sem, vmem_limit_bytes=VMEM_LIMIT)


def _const_spec(shape):
    nd = len(shape)
    return pl.BlockSpec(shape, lambda *_: (0,) * nd)


def _mod_kernel(c_ref, w_ref, b_ref, o_ref):
    c = c_ref[...]
    s = c * jax.nn.sigmoid(c)
    s_hi, s_lo = _split_hi_lo(s)
    w = w_ref[0]
    w_hi, w_lo = _split_hi_lo(w)
    acc = jnp.dot(s_hi, w_hi, preferred_element_type=F32)
    acc += jnp.dot(s_hi, w_lo, preferred_element_type=F32)
    acc += jnp.dot(s_lo, w_hi, preferred_element_type=F32)
    o_ref[0] = acc + b_ref[0]


def _modulation(cvec, w_mod, b_mod):
    depth, d, n = w_mod.shape
    rows = cvec.shape[0]
    tn = 512
    return pl.pallas_call(
        _mod_kernel,
        grid=(depth, n // tn),
        in_specs=[
            pl.BlockSpec((rows, d), lambda l, j: (0, 0)),
            pl.BlockSpec((1, d, tn), lambda l, j: (l, 0, j)),
            pl.BlockSpec((1, 1, tn), lambda l, j: (l, 0, j)),
        ],
        out_specs=pl.BlockSpec((1, rows, tn), lambda l, j: (l, 0, j)),
        out_shape=jax.ShapeDtypeStruct((depth, rows, n), F32),
        compiler_params=_cparams(("parallel", "parallel")),
        name="adaln_modulation",
    )(cvec, w_mod, b_mod.reshape(depth, 1, n))


def _proj0_kernel(nt, cx_ref, xc_ref, xp_ref, xn_ref, mod_ref, gain_ref, mu_ref, wr_ref, wk_ref, wv_ref, g1_ref,
                  g2_ref, w1_ref, w2_ref, w0_ref, a1_ref, a2_ref, a0_ref, kk_ref, ka_ref, rk_ref, seg_ref, segt_ref,
                  trif_ref, trib_ref,
                  v_out, g_out, bonus_out, af_out, rf_out, bf_out, kf_out, ab_out, rb_out, bb_out, kb_out,
                  dlf_out, dlb_out, hs_ref):
    i = pl.program_id(1)
    d = xc_ref.shape[-1]
    q = d // 4
    gain = gain_ref[...]
    shift = mod_ref[0, 0, 0:1, :]
    scale = mod_ref[0, 0, 1:2, :]
    h = _norm_mod(jnp.where(i == 0, cx_ref[0], xc_ref[0]), gain, shift, scale)
    t_idx = lax.broadcasted_iota(jnp.int32, (TM, 1), 0)

    @pl.when(i == 0)
    def _():
        prev = jnp.where(t_idx == 0, 0.0, pltpu.roll(h[:, :2 * q], 1, 0))
        nxt = jnp.where(t_idx == TM - 1, 0.0, pltpu.roll(h[:, 2 * q:], TM - 1, 0))
        hs_ref[:, :2 * q] = prev
        hs_ref[:, 2 * q:] = nxt

    @pl.when(i > 0)
    def _():
        col = t_idx % GRID_W
        left = jnp.where(col == 0, 0.0, pltpu.roll(h[:, :q], 1, 0))
        right = jnp.where(col == GRID_W - 1, 0.0, pltpu.roll(h[:, q:2 * q], TM - 1, 0))
        hp = _norm_mod(xp_ref[0], gain, shift, scale)[:, 2 * q:3 * q]
        hn = _norm_mod(xn_ref[0], gain, shift, scale)[:, 3 * q:]
        hp = jnp.where(i > 1, hp, 0.0)
        hn = jnp.where(i < nt - 1, hn, 0.0)
        hs_ref[:, :q] = left
        hs_ref[:, q:2 * q] = right
        hs_ref[:, 2 * q:3 * q] = jnp.concatenate([hp, h[:TM - GRID_W, 2 * q:3 * q]], axis=0)
        hs_ref[:, 3 * q:] = jnp.concatenate([h[GRID_W:, 3 * q:], hn], axis=0)

    hb = h.astype(BF16)
    xxb = (hs_ref[...] - h).astype(BF16)
    mub = mu_ref[...].astype(BF16)
    mix = lambda j: hb + xxb * mub[j:j + 1, :]
    seg = seg_ref[...]
    segt = segt_ref[...]

    r = jnp.dot(mix(0), wr_ref[...], preferred_element_type=F32)
    k = jnp.dot(mix(2), wk_ref[...], preferred_element_type=F32)
    v = jnp.dot(mix(3), wv_ref[...], preferred_element_type=F32)
    g = _bdot(_sigmoid(jnp.dot(mix(5), g1_ref[...], preferred_element_type=F32)), g2_ref[...])
    wl = _bdot(jnp.tanh(jnp.dot(mix(1), w1_ref[...], preferred_element_type=F32)), w2_ref[...]) + w0_ref[...]
    al = _bdot(jnp.dot(mix(4), a1_ref[...], preferred_element_type=F32), a2_ref[...]) + a0_ref[...]
    v_out[0] = v.astype(BF16)
    g_out[0] = g.astype(BF16)

    kk = k * kk_ref[...]
    kk = kk * lax.rsqrt(_seg_sum(kk * kk, seg, segt, split_input=False) + 1e-12)

    outs = ((af_out, rf_out, bf_out, kf_out, dlf_out, trif_ref), (ab_out, rb_out, bb_out, kb_out, dlb_out, trib_ref))
    ksum = None
    for dr, (a_o, r_o, b_o, k_o, dl_o, tri_ref) in enumerate(outs):
        e = EXP_NEG_HALF * _sigmoid(wl[:, dr * d:(dr + 1) * d])
        a_lr = _sigmoid(al[:, dr * d:(dr + 1) * d])
        k_dir = k * (1.0 + (a_lr - 1.0) * ka_ref[...])
        ksum = k_dir if ksum is None else ksum + k_dir
        e_hi, e_lo = _split_hi_lo(e)
        tri = tri_ref[...]
        c = jnp.dot(tri, e_hi, preferred_element_type=F32) + jnp.dot(tri, e_lo, preferred_element_type=F32)
        grow = jnp.exp(c)
        shrink = jnp.exp(-c)
        a_o[0] = (-kk * jnp.exp(e - c)).astype(BF16)
        r_o[0] = (r * shrink).astype(BF16)
        b_o[0] = (kk * a_lr * grow).astype(BF16)
        k_o[0] = (k_dir * grow).astype(BF16)
        last = CHUNK - 1 if dr == 0 else 0
        dl_o[0, 0] = jnp.concatenate(
            [shrink[n * CHUNK + last:n * CHUNK + last + 1, :] for n in range(TM // CHUNK)], axis=0)

    bonus_out[0] = (_seg_sum(r * ksum * rk_ref[...], seg, segt) * v).astype(BF16)


def _proj0(ctx, x, mods, gain, p, consts):
    b, s, d = x.shape
    t = s + ctx.shape[1]
    nt = t // TM
    hb = TM // GRID_W
    nhb = s // GRID_W
    tok = lambda bb, i: (bb, i, 0)
    big = jax.ShapeDtypeStruct((b, t, d), BF16)
    dl = jax.ShapeDtypeStruct((b, nt, TM // CHUNK, d), F32)
    tile_spec = pl.BlockSpec((1, TM, d), tok)
    dl_spec = pl.BlockSpec((1, 1, TM // CHUNK, d), lambda bb, i: (bb, i, 0, 0))
    weights = [p["mu"], p["wr"], p["wk"], p["wv"], p["g1"], p["g2"], p["w1"], p["w2"], p["w0"],
               p["a1"], p["a2"], p["a0"], p["k_k"], p["k_a"], p["r_k"],
               consts["seg"], consts["segt"], consts["tri_f"], consts["tri_b"]]
    return pl.pallas_call(
        functools.partial(_proj0_kernel, nt),
        grid=(b, nt),
        in_specs=[
            pl.BlockSpec((1, TM, d), lambda bb, i: (bb, 0, 0)),
            pl.BlockSpec((1, TM, d), lambda bb, i: (bb, jnp.maximum(i - 1, 0), 0)),
            pl.BlockSpec((1, GRID_W, d), lambda bb, i: (bb, jnp.maximum((i - 1) * hb - 1, 0), 0)),
            pl.BlockSpec((1, GRID_W, d), lambda bb, i: (bb, jnp.minimum(i * hb, nhb - 1), 0)),
            pl.BlockSpec((1, 1, 6, d), lambda bb, i: (bb, jnp.minimum(i, 1), 0, 0)),
            _const_spec(gain.shape),
        ] + [_const_spec(w.shape) for w in weights],
        out_specs=[tile_spec] * 11 + [dl_spec] * 2,
        out_shape=[big] * 11 + [dl] * 2,
        scratch_shapes=[pltpu.VMEM((TM, d), F32)],
        compiler_params=_cparams(("parallel", "parallel")),
        name="rwkv_project",
    )(ctx, x, x, x, mods, gain, *weights)


_NT = (((1,), (1,)), ((), ()))
_TN = (((0,), (0,)), ((), ()))
PAIR = 2 * HEAD


def _pair_masks(rows):
    lane = lax.broadcasted_iota(jnp.int32, (rows, PAIR), 1)
    return lane < HEAD, lane >= HEAD


def _tri_mask(reverse, inclusive):
    row = lax.broadcasted_iota(jnp.int32, (CHUNK, PAIR), 0)
    col = lax.broadcasted_iota(jnp.int32, (CHUNK, PAIR), 1) % HEAD
    if inclusive:
        return (col >= row) if reverse else (col <= row)
    return (col > row) if reverse else (col < row)


def _wkv_intra_kernel(reverse, a_ref, b_ref, k_ref, v_ref, w_out, u_out):
    npairs = a_ref.shape[-1] // PAIR
    h0, h1 = _pair_masks(CHUNK)
    lane2 = lax.broadcasted_iota(jnp.int32, (CHUNK, 2 * PAIR), 1)
    row2 = lax.broadcasted_iota(jnp.int32, (CHUNK, 2 * PAIR), 0)
    first = (lane2 // HEAD) % 2 == 0
    left = lane2 < PAIR
    col2 = lane2 % HEAD
    strict = (col2 > row2) if reverse else (col2 < row2)
    eye_right = ((col2 == row2) & (lane2 >= PAIR)).astype(F32)
    stack2 = lambda t: jnp.concatenate([jnp.where(h0, t, 0), jnp.where(h1, t, 0)], axis=0)

    per_iter = 2

    def chunk(ci, carry):
        rows = [pl.ds(pl.multiple_of((ci * per_iter + n) * CHUNK, CHUNK), CHUNK) for n in range(per_iter)]
        psl = lambda p: slice(p * PAIR, (p + 1) * PAIR)
        slabs = [(n, p) for n in range(per_iter) for p in range(npairs)]
        a = [a_ref[0, rows[n], psl(p)] for n, p in slabs]
        v = [v_ref[0, rows[n], psl(p)] for n, p in slabs]
        bk = [jnp.concatenate([stack2(b_ref[0, rows[n], psl(p)]), stack2(k_ref[0, rows[n], psl(p)])], axis=0)
              for n, p in slabs]
        sc = [jnp.where(strict, lax.dot_general(a[g], bk[g], _NT, preferred_element_type=F32), 0.0)
              for g in range(len(slabs))]
        x = [jnp.where(left, t, eye_right) for t in sc]
        for _ in range(6):
            xb = [t.astype(BF16) for t in x]
            wts = [jnp.concatenate([jnp.where(first, t, 0), jnp.where(first, 0, t)], axis=0) for t in xb]
            res = [jnp.dot(t[:, :PAIR], w, preferred_element_type=F32) for t, w in zip(xb, wts)]
            x = [r + jnp.where(left, 0.0, t) for r, t in zip(res, x)]
        tinv = [t[:, PAIR:].astype(BF16) for t in x]
        akv = [jnp.dot(sc[g][:, PAIR:].astype(BF16), stack2(v[g]), preferred_element_type=F32).astype(BF16)
               for g in range(len(slabs))]
        wu = [jnp.dot(tinv[g], jnp.concatenate([stack2(a[g]), stack2(akv[g])], axis=1),
                      preferred_element_type=F32) for g in range(len(slabs))]
        for n in range(per_iter):
            mine = wu[n * npairs:(n + 1) * npairs]
            w_out[0, rows[n], :] = jnp.concatenate([t[:, :PAIR] for t in mine], axis=1).astype(BF16)
            u_out[0, rows[n], :] = jnp.concatenate([t[:, PAIR:] for t in mine], axis=1)
        return carry

    lax.fori_loop(0, a_ref.shape[1] // (CHUNK * per_iter), chunk, 0)


def _wkv_intra(reverse, a, bm, k, v):
    b, t, d = a.shape
    tile = pl.BlockSpec((1, TM, d), lambda bb, i: (bb, i, 0))
    return pl.pallas_call(
        functools.partial(_wkv_intra_kernel, reverse),
        grid=(b, t // TM),
        in_specs=[tile] * 4,
        out_specs=[tile, tile],
        out_shape=[jax.ShapeDtypeStruct((b, t, d), BF16), jax.ShapeDtypeStruct((b, t, d), F32)],
        compiler_params=_cparams(("parallel", "parallel")),
        name="wkv_intra_bwd" if reverse else "wkv_intra_fwd",
    )(a, bm, k, v)


def _wkv_scan_kernel(wf, uf, rf, bf, kf, vf, dlf, wb, ub, rb, bb, kb, vb, dlb, yf_ref, yb_ref, z_ref):
    dirs = ((False, wf, uf, rf, bf, kf, vf, dlf, yf_ref), (True, wb, ub, rb, bb, kb, vb, dlb, yb_ref))
    npairs = wf.shape[-1] // PAIR
    nchunks = wf.shape[1] // CHUNK
    slabs = [(dd, p) for dd in range(2) for p in range(npairs)]
    h0, h1 = _pair_masks(CHUNK)
    stack2 = lambda t: jnp.concatenate([jnp.where(h0, t, 0), jnp.where(h1, t, 0)], axis=0)
    lane2 = lax.broadcasted_iota(jnp.int32, (CHUNK, 2 * PAIR), 1) % HEAD
    row2 = lax.broadcasted_iota(jnp.int32, (CHUNK, 2 * PAIR), 0)
    incl = (lane2 <= row2, lane2 >= row2)
    brow = lax.broadcasted_iota(jnp.int32, (PAIR, PAIR), 0) // HEAD
    bcol = lax.broadcasted_iota(jnp.int32, (PAIR, PAIR), 1) // HEAD
    same_head = brow == bcol
    psl = lambda p: slice(p * PAIR, (p + 1) * PAIR)

    @pl.when(pl.program_id(1) == 0)
    def _():
        z_ref[...] = jnp.zeros(z_ref.shape, F32)

    z = [z_ref[dd, p] for dd, p in slabs]
    for n in range(nchunks):
        rows = [slice((nchunks - 1 - n) * CHUNK, (nchunks - n) * CHUNK) if dirs[dd][0]
                else slice(n * CHUNK, (n + 1) * CHUNK) for dd, _ in slabs]
        cidx = [(nchunks - 1 - n) if dirs[dd][0] else n for dd, _ in slabs]
        ld = lambda pos, g: dirs[slabs[g][0]][pos][0, rows[g], psl(slabs[g][1])]
        ng = len(slabs)
        r = [ld(3, g) for g in range(ng)]
        v = [ld(6, g) for g in range(ng)]
        bm = [ld(4, g) for g in range(ng)]
        km = [ld(5, g) for g in range(ng)]
        bk = [jnp.concatenate([bm[g], km[g]], axis=0) for g in range(ng)]
        bk4 = [jnp.concatenate([stack2(bm[g]), stack2(km[g])], axis=0) for g in range(ng)]
        wr = [jnp.concatenate([ld(1, g), r[g]], axis=0) for g in range(ng)]
        sr = [jnp.where(incl[slabs[g][0]], lax.dot_general(r[g], bk4[g], _NT, preferred_element_type=F32),
                        0.0).astype(BF16) for g in range(ng)]
        ws = [lax.dot_general(wr[g], z[g].astype(BF16), _NT, preferred_element_type=F32) for g in range(ng)]
        ub = [(ws[g][:CHUNK] + ld(2, g)).astype(BF16) for g in range(ng)]
        uv = [jnp.concatenate([ub[g], v[g]], axis=0) for g in range(ng)]
        uv4 = [jnp.concatenate([stack2(ub[g]), stack2(v[g])], axis=0) for g in range(ng)]
        ys = [ws[g][CHUNK:] + jnp.dot(sr[g], uv4[g], preferred_element_type=F32) for g in range(ng)]
        upd = [lax.dot_general(uv[g], bk[g], _TN, preferred_element_type=F32) for g in range(ng)]
        z = [(z[g] + jnp.where(same_head, upd[g], 0.0))
             * dirs[slabs[g][0]][7][0, 0, cidx[g]:cidx[g] + 1, psl(slabs[g][1])] for g in range(ng)]
        for dd in range(2):
            dirs[dd][8][0, rows[dd * npairs], :] = jnp.concatenate(ys[dd * npairs:(dd + 1) * npairs], axis=1)
    for g, (dd, p) in enumerate(slabs):
        z_ref[dd, p] = z[g]


def _wkv_scan(ctx_tiles, fwd, bwd):
    b, t, d = fwd[0].shape
    nt = t // TM
    blk_b = lambda j: jnp.where(j < ctx_tiles, ctx_tiles - 1 - j, nt - 1 - j + ctx_tiles)
    specs = []
    for blk in (lambda j: j, blk_b):
        tile = pl.BlockSpec((1, TM, d), lambda bb, j, blk=blk: (bb, blk(j), 0))
        dl_spec = pl.BlockSpec((1, 1, TM // CHUNK, d), lambda bb, j, blk=blk: (bb, blk(j), 0, 0))
        specs.append((tile, dl_spec))
    return pl.pallas_call(
        _wkv_scan_kernel,
        grid=(b, nt),
        in_specs=[specs[0][0]] * 6 + [specs[0][1]] + [specs[1][0]] * 6 + [specs[1][1]],
        out_specs=[specs[0][0], specs[1][0]],
        out_shape=[jax.ShapeDtypeStruct((b, t, d), F32)] * 2,
        scratch_shapes=[pltpu.VMEM((2, d // PAIR, PAIR, PAIR), F32)],
        compiler_params=_cparams(("parallel", "arbitrary")),
        name="wkv_scan",
    )(*fwd, *bwd)


def _mlp(h, w1_ref, w2_ref):
    hb = h.astype(BF16)
    dff = w1_ref.shape[1]
    fc = 1024
    acc = None
    for j in range(dff // fc):
        a = jnp.dot(hb, w1_ref[:, j * fc:(j + 1) * fc], preferred_element_type=F32)
        a = jnp.square(jnp.maximum(a, 0.0)).astype(BF16)
        part = jnp.dot(a, w2_ref[j * fc:(j + 1) * fc, :], preferred_element_type=F32)
        acc = part if acc is None else acc + part
    return acc


def _readout0_kernel(cx_ref, x_ref, yf_ref, yb_ref, bonus_ref, g_ref, mod_ref, lnw_ref, lnb_ref, wo_ref, gain_ref,
                     w1_ref, w2_ref, seg_ref, segt_ref, o_ref):
    x_in = jnp.where(pl.program_id(1) == 0, cx_ref[0], x_ref[0])
    seg = seg_ref[...]
    segt = segt_ref[...]
    y = yf_ref[0] + yb_ref[0]
    mean = _seg_sum(y, seg, segt) * (1.0 / HEAD)
    dev = y - mean
    var = _seg_sum(dev * dev, seg, segt) * (1.0 / HEAD)
    yn = dev * lax.rsqrt(var + GN_EPS) * lnw_ref[...] + lnb_ref[...]
    mixed = (yn + bonus_ref[0].astype(F32)) * g_ref[0].astype(F32)
    o = _bdot(mixed, wo_ref[...])
    x1 = x_in + mod_ref[0, 0, 2:3, :] * o
    h2 = _norm_mod(x1, gain_ref[...], mod_ref[0, 0, 3:4, :], mod_ref[0, 0, 4:5, :])
    o_ref[0] = x1 + mod_ref[0, 0, 5:6, :] * _mlp(h2, w1_ref, w2_ref)


def _readout0(ctx, x, yf, yb, bonus, g, mods, p, gain_mlp, w1, w2, consts):
    b, t, d = yf.shape
    nt = t // TM
    tile = pl.BlockSpec((1, TM, d), lambda bb, i: (bb, i, 0))
    weights = [p["ln_w"], p["ln_b"], p["wo"], gain_mlp, w1, w2, consts["seg"], consts["segt"]]
    return pl.pallas_call(
        _readout0_kernel,
        grid=(b, nt),
        in_specs=[pl.BlockSpec((1, TM, d), lambda bb, i: (bb, 0, 0)),
                  pl.BlockSpec((1, TM, d), lambda bb, i: (bb, jnp.maximum(i - 1, 0), 0))]
        + [tile] * 4 + [pl.BlockSpec((1, 1, 6, d), lambda bb, i: (bb, jnp.minimum(i, 1), 0, 0))]
        + [_const_spec(w.shape) for w in weights],
        out_specs=tile,
        out_shape=jax.ShapeDtypeStruct((b, t, d), F32),
        compiler_params=_cparams(("parallel", "parallel")),
        name="rwkv_readout_mlp",
    )(ctx, x, yf, yb, bonus, g, mods, *weights)


def _rope(x, cos, sin_signed):
    lane = lax.broadcasted_iota(jnp.int32, (1, 128), 1)
    low = (lane & 16) == 0
    cols = []
    for j in range(x.shape[1] // 128):
        xj = x[:, j * 128:(j + 1) * 128]
        partner = jnp.where(low, pltpu.roll(xj, 128 - 16, 1), pltpu.roll(xj, 16, 1))
        cols.append(xj * cos + partner * sin_signed)
    return jnp.concatenate(cols, axis=1)


def _qkv1_kernel(ctx_tiles, x_ref, mod_ref, gain_ref, wq_ref, wk_ref, wvt_ref, qn_ref, kn_ref, cos_ref, sin_ref,
                 seg_ref, segt_ref, segk_ref, segkt_ref, q_out, k_out, vt_out):
    hb = _norm_mod(x_ref[0], gain_ref[...], mod_ref[0, 0, 0:1, :], mod_ref[0, 0, 1:2, :]).astype(BF16)
    cos = cos_ref[...]
    sin = sin_ref[...]

    q = jnp.dot(hb, wq_ref[...], preferred_element_type=F32)
    q = q * lax.rsqrt(_seg_sum(q * q, seg_ref[...], segt_ref[...]) * (1.0 / HEAD) + NORM_EPS) * qn_ref[...]
    q_out[0] = (_rope(q, cos, sin) * (LOG2E * HEAD ** -0.5)).astype(BF16)

    k = jnp.dot(hb, wk_ref[...], preferred_element_type=F32)
    vt = lax.dot_general(wvt_ref[...], hb, _NT, preferred_element_type=F32).astype(BF16)
    k = k * lax.rsqrt(_seg_sum(k * k, segk_ref[...], segkt_ref[...]) * (1.0 / HEAD) + NORM_EPS) * kn_ref[...]
    k = _rope(k, cos, sin).astype(BF16)
    ones = jnp.ones((VT_ROWS - HEAD, vt.shape[1]), BF16)
    for gi in range(KV_HEADS):
        k_out[0, gi] = k[:, gi * HEAD:(gi + 1) * HEAD]
        vt_out[0, gi] = jnp.concatenate([vt[gi * HEAD:(gi + 1) * HEAD, :], ones], axis=0)


def _qkv1(x2, mods, gain, wq, wk, wvt, qn, kn, cos, sin, consts, ctx_tiles):
    b, t, d = x2.shape
    nt = t // TM
    tile = pl.BlockSpec((1, TM, d), lambda bb, i: (bb, i, 0))
    tab = pl.BlockSpec((TM, 128), lambda bb, i: (i, 0))
    weights = [gain, wq, wk, wvt, qn, kn]
    segs = [consts["seg"], consts["segt"], consts["segk"], consts["segkt"]]
    return pl.pallas_call(
        functools.partial(_qkv1_kernel, ctx_tiles),
        grid=(b, nt),
        in_specs=[tile, pl.BlockSpec((1, 1, 6, d), lambda bb, i: (bb, jnp.minimum(i, 1), 0, 0))]
        + [_const_spec(w.shape) for w in weights] + [tab, tab] + [_const_spec(w.shape) for w in segs],
        out_specs=[pl.BlockSpec((1, TM, d), lambda bb, i: (bb, jnp.maximum(i - ctx_tiles, 0), 0)),
                   pl.BlockSpec((1, KV_HEADS, TM, HEAD), lambda bb, i: (bb, 0, i, 0)),
                   pl.BlockSpec((1, KV_HEADS, VT_ROWS, TM), lambda bb, i: (bb, 0, 0, i))],
        out_shape=[jax.ShapeDtypeStruct((b, t - ctx_tiles * TM, d), BF16),
                   jax.ShapeDtypeStruct((b, KV_HEADS, t, HEAD), BF16),
                   jax.ShapeDtypeStruct((b, KV_HEADS, VT_ROWS, t), BF16)],
        compiler_params=_cparams(("parallel", "arbitrary")),
        name="attn_qkv",
    )(x2, mods, *weights, cos, sin, *segs)


def _attn_kernel(q_ref, k_ref, vt_ref, o_ref):
    nheads = q_ref.shape[-1] // HEAD
    nkeys = k_ref.shape[2]
    bounds = list(range(0, nkeys, KEY_BLOCK)) + [nkeys]
    nblk = len(bounds) - 1
    q = [q_ref[0, :, j * HEAD:(j + 1) * HEAD] for j in range(nheads)]

    def scores(i):
        kb = k_ref[0, 0, bounds[i]:bounds[i + 1], :]
        return [lax.dot_general(kb, qj, _NT, preferred_element_type=F32) for qj in q]

    m = [None] * nheads
    acc = [None] * nheads
    s_next = scores(0)
    for i in range(nblk):
        s = s_next
        if i + 1 < nblk:
            s_next = scores(i + 1)
        vb = vt_ref[0, 0, :, bounds[i]:bounds[i + 1]]
        for j in range(nheads):
            bmax = jnp.max(s[j], axis=0, keepdims=True)
            m_new = bmax if i == 0 else jnp.maximum(m[j], bmax)
            p = jnp.exp2(s[j] - m_new).astype(BF16)
            pv = jnp.dot(vb, p, preferred_element_type=F32)
            acc[j] = pv if i == 0 else acc[j] * jnp.exp2(m[j] - m_new) + pv
            m[j] = m_new
    outs = [a[:HEAD] / a[HEAD:HEAD + 1] for a in acc]
    o_ref[0] = jnp.concatenate(outs, axis=0).T.astype(BF16)


def _attention(q, k, vt):
    b, s, d = q.shape
    t = k.shape[2]
    gw = d // KV_HEADS
    return pl.pallas_call(
        _attn_kernel,
        grid=(b, KV_HEADS, s // TQ),
        in_specs=[pl.BlockSpec((1, TQ, gw), lambda bb, gi, i: (bb, i, gi)),
                  pl.BlockSpec((1, 1, t, HEAD), lambda bb, gi, i: (bb, gi, 0, 0)),
                  pl.BlockSpec((1, 1, VT_ROWS, t), lambda bb, gi, i: (bb, gi, 0, 0))],
        out_specs=pl.BlockSpec((1, TQ, gw), lambda bb, gi, i: (bb, i, gi)),
        out_shape=jax.ShapeDtypeStruct((b, s, d), BF16),
        compiler_params=_cparams(("parallel", "parallel", "parallel")),
        name="gqa_attention",
    )(q, k, vt)


def _out1_kernel(x_ref, a_ref, mod_ref, wo_ref, gain_ref, w1_ref, w2_ref, fin_ref, o_ref):
    o = jnp.dot(a_ref[0], wo_ref[...], preferred_element_type=F32)
    x1 = x_ref[0] + mod_ref[0, 0, 2:3, :] * o
    h2 = _norm_mod(x1, gain_ref[...], mod_ref[0, 0, 3:4, :], mod_ref[0, 0, 4:5, :])
    x2 = x1 + mod_ref[0, 0, 5:6, :] * _mlp(h2, w1_ref, w2_ref)
    ms = jnp.mean(x2 * x2, axis=-1, keepdims=True)
    o_ref[0] = x2 * lax.rsqrt(ms + NORM_EPS) * fin_ref[...]


def _out1(x2, att, mods, wo, gain_mlp, w1, w2, fin, ctx_tiles):
    b, s, d = att.shape
    weights = [wo, gain_mlp, w1, w2, fin]
    return pl.pallas_call(
        _out1_kernel,
        grid=(b, s // TM),
        in_specs=[pl.BlockSpec((1, TM, d), lambda bb, i: (bb, i + ctx_tiles, 0)),
                  pl.BlockSpec((1, TM, d), lambda bb, i: (bb, i, 0)),
                  pl.BlockSpec((1, 1, 6, d), lambda bb, i: (bb, 1, 0, 0))]
        + [_const_spec(w.shape) for w in weights],
        out_specs=pl.BlockSpec((1, TM, d), lambda bb, i: (bb, i, 0)),
        out_shape=jax.ShapeDtypeStruct((b, s, d), F32),
        compiler_params=_cparams(("parallel", "parallel")),
        name="attn_out_mlp_final",
    )(x2, att, mods, *weights)


def _block_diag2(a, b):
    za = jnp.zeros((a.shape[0], b.shape[1]), a.dtype)
    zb = jnp.zeros((b.shape[0], a.shape[1]), b.dtype)
    return jnp.concatenate([jnp.concatenate([a, za], 1), jnp.concatenate([zb, b], 1)], 0)


def _constants(d):
    lane = jnp.arange(d)[:, None] // HEAD
    seg = (lane == jnp.arange(SEG_W)[None, :]).astype(BF16)
    nk = KV_HEADS * HEAD
    segk = seg[:nk]
    t = jnp.arange(TM)
    same = (t[:, None] // CHUNK) == (t[None, :] // CHUNK)
    tri_f = (same & (t[None, :] <= t[:, None])).astype(BF16)
    tri_b = (same & (t[None, :] >= t[:, None])).astype(BF16)
    return dict(seg=seg, segt=seg.T, segk=segk, segkt=segk.T, tri_f=tri_f, tri_b=tri_b)


def _rope_tables(s, c_len):
    tok = jnp.arange(s)
    row = (tok // GRID_W).astype(F32)
    col = (tok % GRID_W).astype(F32)
    half = HEAD // 2
    freqs = ROPE_THETA ** (-jnp.arange(0, half, 2, dtype=F32) / half)
    ang_r = row[:, None] * freqs
    ang_c = col[:, None] * freqs
    cos = jnp.concatenate([jnp.cos(ang_r)] * 2 + [jnp.cos(ang_c)] * 2, axis=1)
    sin = jnp.concatenate([-jnp.sin(ang_r), jnp.sin(ang_r), -jnp.sin(ang_c), jnp.sin(ang_c)], axis=1)
    cos = jnp.concatenate([jnp.ones((c_len, HEAD), F32), cos], axis=0)
    sin = jnp.concatenate([jnp.zeros((c_len, HEAD), F32), sin], axis=0)
    return jnp.tile(cos, (1, 2)), jnp.tile(sin, (1, 2))


def kernel(x, c, ctx, c_ctx, w_mod, b_mod, norm_mix, norm_mlp, mlp_w1, mlp_w2, rwkv_mu, rwkv_wr, rwkv_wk, rwkv_wv,
           rwkv_wo, rwkv_w0, rwkv_w1, rwkv_w2, rwkv_a0, rwkv_a1, rwkv_a2, rwkv_g1, rwkv_g2, rwkv_k_k, rwkv_k_a,
           rwkv_r_k, rwkv_ln_w, rwkv_ln_b, attn_wqkv, attn_q_norm, attn_k_norm, attn_wo, final_norm):
    b, s, d = x.shape
    c_len = ctx.shape[1]
    assert d % (4 * 128) == 0 and d // HEAD <= SEG_W
    assert s % TQ == 0 and TQ % TM == 0 and TM % GRID_W == 0 and TM % CHUNK == 0
    assert c_len == TM, "the context must fill exactly one token tile"
    assert w_mod.shape[0] == 2 and rwkv_mu.shape[0] == 1 and attn_wqkv.shape[0] == 1
    ctx_tiles = c_len // TM
    row = lambda a: a.reshape(1, -1).astype(F32)
    bf = lambda a: a.astype(BF16)
    consts = _constants(d)

    rows = -(-(b + 1) // 8) * 8
    cvec = jnp.zeros((rows, d), F32).at[:b].set(c).at[b].set(c_ctx)
    m = _modulation(cvec, w_mod, b_mod)
    m = m.reshape(2, rows, 6, d)
    mods = [jnp.stack([jnp.broadcast_to(m[l, b][None], (b, 6, d)), m[l, :b]], axis=1) for l in range(2)]

    p0 = dict(
        mu=rwkv_mu[0], wr=bf(rwkv_wr[0]), wk=bf(rwkv_wk[0]), wv=bf(rwkv_wv[0]), wo=bf(rwkv_wo[0]),
        g1=bf(rwkv_g1[0]), g2=bf(rwkv_g2[0]),
        w1=bf(jnp.concatenate([rwkv_w1[0, 0], rwkv_w1[0, 1]], axis=1)),
        w2=bf(_block_diag2(rwkv_w2[0, 0], rwkv_w2[0, 1])),
        w0=rwkv_w0[0].reshape(1, -1),
        a1=bf(jnp.concatenate([rwkv_a1[0, 0], rwkv_a1[0, 1]], axis=1)),
        a2=bf(_block_diag2(rwkv_a2[0, 0], rwkv_a2[0, 1])),
        a0=rwkv_a0[0].reshape(1, -1),
        k_k=row(rwkv_k_k[0]), k_a=row(rwkv_k_a[0]), r_k=row(rwkv_r_k[0]),
        ln_w=row(rwkv_ln_w[0]), ln_b=row(rwkv_ln_b[0]),
    )
    (v, g, bonus, a_f, r_f, b_f, k_f, a_b, r_b, b_b, k_b, dl_f, dl_b) = _proj0(
        ctx, x, mods[0], row(norm_mix[0]), p0, consts)
    w_f, u_f = _wkv_intra(False, a_f, b_f, k_f, v)
    w_b, u_b = _wkv_intra(True, a_b, b_b, k_b, v)
    y_f, y_b = _wkv_scan(ctx_tiles, (w_f, u_f, r_f, b_f, k_f, v, dl_f), (w_b, u_b, r_b, b_b, k_b, v, dl_b))
    x1 = _readout0(ctx, x, y_f, y_b, bonus, g, mods[0], p0, row(norm_mlp[0]), bf(mlp_w1[0]), bf(mlp_w2[0]),
                   consts)

    cos, sin = _rope_tables(s, c_len)
    qn = jnp.tile(attn_q_norm[0], d // HEAD).reshape(1, -1)
    kn = jnp.tile(attn_k_norm[0], KV_HEADS).reshape(1, -1)
    nqk = d + KV_HEADS * HEAD
    wqkv = attn_wqkv[0]
    q, k, vt = _qkv1(x1, mods[1], row(norm_mix[1]), bf(wqkv[:, :d]), bf(wqkv[:, d:nqk]), bf(wqkv[:, nqk:].T),
                     qn, kn, cos, sin, consts, ctx_tiles)
    att = _attention(q, k, vt)
    return _out1(x1, att, mods[1], bf(attn_wo[0]), row(norm_mlp[1]), bf(mlp_w1[1]), bf(mlp_w2[1]),
                 row(final_norm), ctx_tiles)
```

```python
import functools
import math

import jax
import jax.numpy as jnp
from jax import lax
from jax.experimental import pallas as pl
from jax.experimental.pallas import tpu as pltpu

F32 = jnp.float32
BF16 = jnp.bfloat16

HEAD = 64
GRID_W = 64
KV_HEADS = 4
NORM_EPS = 1e-6
GN_EPS = 64e-5
ROPE_THETA = 10000.0
CHUNK = 64
TM = 256
SEG_W = 128
VMEM_LIMIT = 56 * 1024 * 1024
EXP_NEG_HALF = math.exp(-0.5)
LOG2E = math.log2(math.e)
KEY_BLOCK = 256
TQ = 256
VT_ROWS = 2 * HEAD
PROJ_PARTS = 1


def _bdot(a, b):
    return jnp.dot(a.astype(BF16), b.astype(BF16), preferred_element_type=F32)


def _split_hi_lo(x):
    hi = x.astype(BF16)
    lo = (x - hi.astype(F32)).astype(BF16)
    return hi, lo


def _seg_sum(x, seg, segt, split_input=True):
    if split_input:
        hi, lo = _split_hi_lo(x)
        s = jnp.dot(hi, seg, preferred_element_type=F32) + jnp.dot(lo, seg, preferred_element_type=F32)
    else:
        s = jnp.dot(x.astype(BF16), seg, preferred_element_type=F32)
    s_hi, s_lo = _split_hi_lo(s)
    return jnp.dot(s_hi, segt, preferred_element_type=F32) + jnp.dot(s_lo, segt, preferred_element_type=F32)


def _sigmoid(x):
    return 0.5 * jnp.tanh(0.5 * x) + 0.5


def _norm_mod(x, gain, shift, scale):
    ms = jnp.mean(x * x, axis=-1, keepdims=True)
    return x * lax.rsqrt(ms + NORM_EPS) * (gain * (1.0 + scale)) + shift


def _cparams(sem):
    return pltpu.CompilerParams(dimension_semantics=sem, vmem_limit_bytes=VMEM_LIMIT)


def _const_spec(shape):
    nd = len(shape)
    return pl.BlockSpec(shape, lambda *_: (0,) * nd)


def _mod_kernel(c_ref, w_ref, b_ref, o_ref):
    c = c_ref[...]
    s = c * jax.nn.sigmoid(c)
    s_hi, s_lo = _split_hi_lo(s)
    w = w_ref[0]
    w_hi, w_lo = _split_hi_lo(w)
    acc = jnp.dot(s_hi, w_hi, preferred_element_type=F32)
    acc += jnp.dot(s_hi, w_lo, preferred_element_type=F32)
    acc += jnp.dot(s_lo, w_hi, preferred_element_type=F32)
    o_ref[0] = acc + b_ref[0]


def _modulation(cvec, w_mod, b_mod):
    depth, d, n = w_mod.shape
    rows = cvec.shape[0]
    tn = 512
    return pl.pallas_call(
        _mod_kernel,
        grid=(depth, n // tn),
        in_specs=[
            pl.BlockSpec((rows, d), lambda l, j: (0, 0)),
            pl.BlockSpec((1, d, tn), lambda l, j: (l, 0, j)),
            pl.BlockSpec((1, 1, tn), lambda l, j: (l, 0, j)),
        ],
        out_specs=pl.BlockSpec((1, rows, tn), lambda l, j: (l, 0, j)),
        out_shape=jax.ShapeDtypeStruct((depth, rows, n), F32),
        compiler_params=_cparams(("parallel", "parallel")),
        name="adaln_modulation",
    )(cvec, w_mod, b_mod.reshape(depth, 1, n))


def _proj0_kernel(nt, cx_ref, xc_ref, xp_ref, xn_ref, mod_ref, gain_ref, mu_ref, wr_ref, wk_ref, wv_ref, g1_ref,
                  g2_ref, w1_ref, w2_ref, w0_ref, a1_ref, a2_ref, a0_ref, kk_ref, ka_ref, rk_ref, seg_ref, segt_ref,
                  trif_ref, trib_ref,
                  v_out, g_out, bonus_out, af_out, rf_out, bf_out, kf_out, ab_out, rb_out, bb_out, kb_out,
                  dlf_out, dlb_out, hs_ref):
    i = pl.program_id(1)
    d = xc_ref.shape[-1]
    q = d // 4
    gain = gain_ref[...]
    shift = mod_ref[0, 0, 0:1, :]
    scale = mod_ref[0, 0, 1:2, :]
    h = _norm_mod(jnp.where(i == 0, cx_ref[0], xc_ref[0]), gain, shift, scale)
    t_idx = lax.broadcasted_iota(jnp.int32, (TM, 1), 0)

    @pl.when(i == 0)
    def _():
        prev = jnp.where(t_idx == 0, 0.0, pltpu.roll(h[:, :2 * q], 1, 0))
        nxt = jnp.where(t_idx == TM - 1, 0.0, pltpu.roll(h[:, 2 * q:], TM - 1, 0))
        hs_ref[:, :2 * q] = prev
        hs_ref[:, 2 * q:] = nxt

    @pl.when(i > 0)
    def _():
        col = t_idx % GRID_W
        left = jnp.where(col == 0, 0.0, pltpu.roll(h[:, :q], 1, 0))
        right = jnp.where(col == GRID_W - 1, 0.0, pltpu.roll(h[:, q:2 * q], TM - 1, 0))
        hp = _norm_mod(xp_ref[0], gain, shift, scale)[:, 2 * q:3 * q]
        hn = _norm_mod(xn_ref[0], gain, shift, scale)[:, 3 * q:]
        hp = jnp.where(i > 1, hp, 0.0)
        hn = jnp.where(i < nt - 1, hn, 0.0)
        hs_ref[:, :q] = left
        hs_ref[:, q:2 * q] = right
        hs_ref[:, 2 * q:3 * q] = jnp.concatenate([hp, h[:TM - GRID_W, 2 * q:3 * q]], axis=0)
        hs_ref[:, 3 * q:] = jnp.concatenate([h[GRID_W:, 3 * q:], hn], axis=0)

    mub = mu_ref[...].astype(BF16)
    seg = seg_ref[...]
    segt = segt_ref[...]
    pr = TM // PROJ_PARTS
    parts = [slice(n * pr, (n + 1) * pr) for n in range(PROJ_PARTS)]
    proj = []
    for rs in parts:
        hp_ = h[rs]
        hb = hp_.astype(BF16)
        xxb = (hs_ref[rs, :] - hp_).astype(BF16)
        mix = lambda j: hb + xxb * mub[j:j + 1, :]
        r = jnp.dot(mix(0), wr_ref[...], preferred_element_type=F32)
        k = jnp.dot(mix(2), wk_ref[...], preferred_element_type=F32)
        v = jnp.dot(mix(3), wv_ref[...], preferred_element_type=F32)
        g = _bdot(_sigmoid(jnp.dot(mix(5), g1_ref[...], preferred_element_type=F32)), g2_ref[...])
        wl = _bdot(jnp.tanh(jnp.dot(mix(1), w1_ref[...], preferred_element_type=F32)), w2_ref[...]) + w0_ref[...]
        al = _bdot(jnp.dot(mix(4), a1_ref[...], preferred_element_type=F32), a2_ref[...]) + a0_ref[...]
        proj.append((r, k, v, g, wl, al))

    sums = []
    for rs, (r, k, v, g, wl, al) in zip(parts, proj):
        v_out[0, rs, :] = v.astype(BF16)
        g_out[0, rs, :] = g.astype(BF16)
        kk0 = k * kk_ref[...]
        norm = _seg_sum(kk0 * kk0, seg, segt, split_input=False)
        e = EXP_NEG_HALF * _sigmoid(wl)
        e_hi, e_lo = _split_hi_lo(e)
        cs = []
        for dr, tri_ref in enumerate((trif_ref, trib_ref)):
            tri = tri_ref[:pr, :pr]
            cols = slice(dr * d, (dr + 1) * d)
            cs.append(jnp.dot(tri, e_hi[:, cols], preferred_element_type=F32)
                      + jnp.dot(tri, e_lo[:, cols], preferred_element_type=F32))
        sums.append((kk0, norm, e, cs))

    outs = ((af_out, rf_out, bf_out, kf_out, dlf_out), (ab_out, rb_out, bb_out, kb_out, dlb_out))
    cpp = pr // CHUNK
    for n, (rs, (r, k, v, g, wl, al), (kk0, norm, e, cs)) in enumerate(zip(parts, proj, sums)):
        kk = kk0 * lax.rsqrt(norm + 1e-12)
        ksum = None
        for dr, (a_o, r_o, b_o, k_o, dl_o) in enumerate(outs):
            cols = slice(dr * d, (dr + 1) * d)
            c = cs[dr]
            a_lr = _sigmoid(al[:, cols])
            k_dir = k * (1.0 + (a_lr - 1.0) * ka_ref[...])
            ksum = k_dir if ksum is None else ksum + k_dir
            grow = jnp.exp(c)
            shrink = jnp.exp(-c)
            a_o[0, rs, :] = (-kk * jnp.exp(e[:, cols] - c)).astype(BF16)
            r_o[0, rs, :] = (r * shrink).astype(BF16)
            b_o[0, rs, :] = (kk * a_lr * grow).astype(BF16)
            k_o[0, rs, :] = (k_dir * grow).astype(BF16)
            last = CHUNK - 1 if dr == 0 else 0
            dl_o[0, 0, n * cpp:(n + 1) * cpp, :] = jnp.concatenate(
                [shrink[m * CHUNK + last:m * CHUNK + last + 1, :] for m in range(cpp)], axis=0)
        bonus_out[0, rs, :] = (_seg_sum(r * ksum * rk_ref[...], seg, segt) * v).astype(BF16)


def _proj0(ctx, x, mods, gain, p, consts):
    b, s, d = x.shape
    t = s + ctx.shape[1]
    nt = t // TM
    hb = TM // GRID_W
    nhb = s // GRID_W
    tok = lambda bb, i: (bb, i, 0)
    big = jax.ShapeDtypeStruct((b, t, d), BF16)
    dl = jax.ShapeDtypeStruct((b, nt, TM // CHUNK, d), F32)
    tile_spec = pl.BlockSpec((1, TM, d), tok)
    dl_spec = pl.BlockSpec((1, 1, TM // CHUNK, d), lambda bb, i: (bb, i, 0, 0))
    weights = [p["mu"], p["wr"], p["wk"], p["wv"], p["g1"], p["g2"], p["w1"], p["w2"], p["w0"],
               p["a1"], p["a2"], p["a0"], p["k_k"], p["k_a"], p["r_k"],
               consts["seg"], consts["segt"], consts["tri_f"], consts["tri_b"]]
    return pl.pallas_call(
        functools.partial(_proj0_kernel, nt),
        grid=(b, nt),
        in_specs=[
            pl.BlockSpec((1, TM, d), lambda bb, i: (bb, 0, 0)),
            pl.BlockSpec((1, TM, d), lambda bb, i: (bb, jnp.maximum(i - 1, 0), 0)),
            pl.BlockSpec((1, GRID_W, d), lambda bb, i: (bb, jnp.maximum((i - 1) * hb - 1, 0), 0)),
            pl.BlockSpec((1, GRID_W, d), lambda bb, i: (bb, jnp.minimum(i * hb, nhb - 1), 0)),
            pl.BlockSpec((1, 1, 6, d), lambda bb, i: (bb, jnp.minimum(i, 1), 0, 0)),
            _const_spec(gain.shape),
        ] + [_const_spec(w.shape) for w in weights],
        out_specs=[tile_spec] * 11 + [dl_spec] * 2,
        out_shape=[big] * 11 + [dl] * 2,
        scratch_shapes=[pltpu.VMEM((TM, d), F32)],
        compiler_params=_cparams(("parallel", "parallel")),
        name="rwkv_project",
    )(ctx, x, x, x, mods, gain, *weights)


_NT = (((1,), (1,)), ((), ()))
_TN = (((0,), (0,)), ((), ()))
PAIR = 2 * HEAD


def _pair_masks(rows):
    lane = lax.broadcasted_iota(jnp.int32, (rows, PAIR), 1)
    return lane < HEAD, lane >= HEAD


def _tri_mask(reverse, inclusive):
    row = lax.broadcasted_iota(jnp.int32, (CHUNK, PAIR), 0)
    col = lax.broadcasted_iota(jnp.int32, (CHUNK, PAIR), 1) % HEAD
    if inclusive:
        return (col >= row) if reverse else (col <= row)
    return (col > row) if reverse else (col < row)


def _wkv_intra_kernel(reverse, a_ref, b_ref, k_ref, v_ref, w_out, u_out):
    npairs = a_ref.shape[-1] // PAIR
    h0, h1 = _pair_masks(CHUNK)
    lane2 = lax.broadcasted_iota(jnp.int32, (CHUNK, 2 * PAIR), 1)
    row2 = lax.broadcasted_iota(jnp.int32, (CHUNK, 2 * PAIR), 0)
    first = (lane2 // HEAD) % 2 == 0
    left = lane2 < PAIR
    col2 = lane2 % HEAD
    strict = (col2 > row2) if reverse else (col2 < row2)
    eye_right = ((col2 == row2) & (lane2 >= PAIR)).astype(F32)
    stack2 = lambda t: jnp.concatenate([jnp.where(h0, t, 0), jnp.where(h1, t, 0)], axis=0)

    per_iter = 2

    def chunk(ci, carry):
        rows = [pl.ds(pl.multiple_of((ci * per_iter + n) * CHUNK, CHUNK), CHUNK) for n in range(per_iter)]
        psl = lambda p: slice(p * PAIR, (p + 1) * PAIR)
        slabs = [(n, p) for n in range(per_iter) for p in range(npairs)]
        a = [a_ref[0, rows[n], psl(p)] for n, p in slabs]
        v = [v_ref[0, rows[n], psl(p)] for n, p in slabs]
        bk = [jnp.concatenate([stack2(b_ref[0, rows[n], psl(p)]), stack2(k_ref[0, rows[n], psl(p)])], axis=0)
              for n, p in slabs]
        sc = [jnp.where(strict, lax.dot_general(a[g], bk[g], _NT, preferred_element_type=F32), 0.0)
              for g in range(len(slabs))]
        x = [jnp.where(left, t, eye_right) for t in sc]
        for _ in range(6):
            xb = [t.astype(BF16) for t in x]
            wts = [jnp.concatenate([jnp.where(first, t, 0), jnp.where(first, 0, t)], axis=0) for t in xb]
            res = [jnp.dot(t[:, :PAIR], w, preferred_element_type=F32) for t, w in zip(xb, wts)]
            x = [r + jnp.where(left, 0.0, t) for r, t in zip(res, x)]
        tinv = [t[:, PAIR:].astype(BF16) for t in x]
        akv = [jnp.dot(sc[g][:, PAIR:].astype(BF16), stack2(v[g]), preferred_element_type=F32).astype(BF16)
               for g in range(len(slabs))]
        wu = [jnp.dot(tinv[g], jnp.concatenate([stack2(a[g]), stack2(akv[g])], axis=1),
                      preferred_element_type=F32) for g in range(len(slabs))]
        for n in range(per_iter):
            mine = wu[n * npairs:(n + 1) * npairs]
            w_out[0, rows[n], :] = jnp.concatenate([t[:, :PAIR] for t in mine], axis=1).astype(BF16)
            u_out[0, rows[n], :] = jnp.concatenate([t[:, PAIR:] for t in mine], axis=1).astype(BF16)
        return carry

    lax.fori_loop(0, a_ref.shape[1] // (CHUNK * per_iter), chunk, 0)


def _wkv_intra(reverse, a, bm, k, v):
    b, t, d = a.shape
    tile = pl.BlockSpec((1, TM, d), lambda bb, i: (bb, i, 0))
    return pl.pallas_call(
        functools.partial(_wkv_intra_kernel, reverse),
        grid=(b, t // TM),
        in_specs=[tile] * 4,
        out_specs=[tile, tile],
        out_shape=[jax.ShapeDtypeStruct((b, t, d), BF16)] * 2,
        compiler_params=_cparams(("parallel", "parallel")),
        name="wkv_intra_bwd" if reverse else "wkv_intra_fwd",
    )(a, bm, k, v)


def _wkv_scan_kernel(wf, uf, rf, bf, kf, vf, dlf, wb, ub, rb, bb, kb, vb, dlb, yf_ref, yb_ref, z_ref):
    dirs = ((False, wf, uf, rf, bf, kf, vf, dlf, yf_ref), (True, wb, ub, rb, bb, kb, vb, dlb, yb_ref))
    npairs = wf.shape[-1] // PAIR
    nchunks = wf.shape[1] // CHUNK
    slabs = [(dd, p) for dd in range(2) for p in range(npairs)]
    h0, h1 = _pair_masks(CHUNK)
    stack2 = lambda t: jnp.concatenate([jnp.where(h0, t, 0), jnp.where(h1, t, 0)], axis=0)
    lane2 = lax.broadcasted_iota(jnp.int32, (CHUNK, 2 * PAIR), 1) % HEAD
    row2 = lax.broadcasted_iota(jnp.int32, (CHUNK, 2 * PAIR), 0)
    incl = (lane2 <= row2, lane2 >= row2)
    brow = lax.broadcasted_iota(jnp.int32, (PAIR, PAIR), 0) // HEAD
    bcol = lax.broadcasted_iota(jnp.int32, (PAIR, PAIR), 1) // HEAD
    same_head = brow == bcol
    psl = lambda p: slice(p * PAIR, (p + 1) * PAIR)

    @pl.when(pl.program_id(1) == 0)
    def _():
        z_ref[...] = jnp.zeros(z_ref.shape, F32)

    z = [z_ref[dd, p] for dd, p in slabs]
    for n in range(nchunks):
        rows = [slice((nchunks - 1 - n) * CHUNK, (nchunks - n) * CHUNK) if dirs[dd][0]
                else slice(n * CHUNK, (n + 1) * CHUNK) for dd, _ in slabs]
        cidx = [(nchunks - 1 - n) if dirs[dd][0] else n for dd, _ in slabs]
        ld = lambda pos, g: dirs[slabs[g][0]][pos][0, rows[g], psl(slabs[g][1])]
        ng = len(slabs)
        r = [ld(3, g) for g in range(ng)]
        v = [ld(6, g) for g in range(ng)]
        bm = [ld(4, g) for g in range(ng)]
        km = [ld(5, g) for g in range(ng)]
        bk = [jnp.concatenate([bm[g], km[g]], axis=0) for g in range(ng)]
        bk4 = [jnp.concatenate([stack2(bm[g]), stack2(km[g])], axis=0) for g in range(ng)]
        wr = [jnp.concatenate([ld(1, g), r[g]], axis=0) for g in range(ng)]
        sr = [jnp.where(incl[slabs[g][0]], lax.dot_general(r[g], bk4[g], _NT, preferred_element_type=F32),
                        0.0).astype(BF16) for g in range(ng)]
        ws = [lax.dot_general(wr[g], z[g].astype(BF16), _NT, preferred_element_type=F32) for g in range(ng)]
        ub = [(ws[g][:CHUNK] + ld(2, g)).astype(BF16) for g in range(ng)]
        uv = [jnp.concatenate([ub[g], v[g]], axis=0) for g in range(ng)]
        uv4 = [jnp.concatenate([stack2(ub[g]), stack2(v[g])], axis=0) for g in range(ng)]
        ys = [ws[g][CHUNK:] + jnp.dot(sr[g], uv4[g], preferred_element_type=F32) for g in range(ng)]
        upd = [lax.dot_general(uv[g], bk[g], _TN, preferred_element_type=F32) for g in range(ng)]
        z = [(z[g] + jnp.where(same_head, upd[g], 0.0))
             * dirs[slabs[g][0]][7][0, 0, cidx[g]:cidx[g] + 1, psl(slabs[g][1])] for g in range(ng)]
        for dd in range(2):
            dirs[dd][8][0, rows[dd * npairs], :] = jnp.concatenate(
                ys[dd * npairs:(dd + 1) * npairs], axis=1).astype(BF16)
    for g, (dd, p) in enumerate(slabs):
        z_ref[dd, p] = z[g]


def _wkv_scan(ctx_tiles, fwd, bwd):
    b, t, d = fwd[0].shape
    nt = t // TM
    blk_b = lambda j: jnp.where(j < ctx_tiles, ctx_tiles - 1 - j, nt - 1 - j + ctx_tiles)
    specs = []
    for blk in (lambda j: j, blk_b):
        tile = pl.BlockSpec((1, TM, d), lambda bb, j, blk=blk: (bb, blk(j), 0))
        dl_spec = pl.BlockSpec((1, 1, TM // CHUNK, d), lambda bb, j, blk=blk: (bb, blk(j), 0, 0))
        specs.append((tile, dl_spec))
    return pl.pallas_call(
        _wkv_scan_kernel,
        grid=(b, nt),
        in_specs=[specs[0][0]] * 6 + [specs[0][1]] + [specs[1][0]] * 6 + [specs[1][1]],
        out_specs=[specs[0][0], specs[1][0]],
        out_shape=[jax.ShapeDtypeStruct((b, t, d), BF16)] * 2,
        scratch_shapes=[pltpu.VMEM((2, d // PAIR, PAIR, PAIR), F32)],
        compiler_params=_cparams(("parallel", "arbitrary")),
        name="wkv_scan",
    )(*fwd, *bwd)


def _mlp(h, w1_ref, w2_ref):
    hb = h.astype(BF16)
    dff = w1_ref.shape[1]
    fc = 1024
    acc = None
    for j in range(dff // fc):
        a = jnp.dot(hb, w1_ref[:, j * fc:(j + 1) * fc], preferred_element_type=F32)
        a = jnp.square(jnp.maximum(a, 0.0)).astype(BF16)
        part = jnp.dot(a, w2_ref[j * fc:(j + 1) * fc, :], preferred_element_type=F32)
        acc = part if acc is None else acc + part
    return acc


def _readout0_kernel(cx_ref, x_ref, yf_ref, yb_ref, bonus_ref, g_ref, mod_ref, lnw_ref, lnb_ref, wo_ref, gain_ref,
                     w1_ref, w2_ref, seg_ref, segt_ref, o_ref):
    x_in = jnp.where(pl.program_id(1) == 0, cx_ref[0], x_ref[0])
    seg = seg_ref[...]
    segt = segt_ref[...]
    y = yf_ref[0].astype(F32) + yb_ref[0].astype(F32)
    mean = _seg_sum(y, seg, segt, split_input=False) * (1.0 / HEAD)
    dev = y - mean
    var = _seg_sum(dev * dev, seg, segt, split_input=False) * (1.0 / HEAD)
    yn = dev * lax.rsqrt(var + GN_EPS) * lnw_ref[...] + lnb_ref[...]
    mixed = (yn + bonus_ref[0].astype(F32)) * g_ref[0].astype(F32)
    o = _bdot(mixed, wo_ref[...])
    x1 = x_in + mod_ref[0, 0, 2:3, :] * o
    h2 = _norm_mod(x1, gain_ref[...], mod_ref[0, 0, 3:4, :], mod_ref[0, 0, 4:5, :])
    o_ref[0] = x1 + mod_ref[0, 0, 5:6, :] * _mlp(h2, w1_ref, w2_ref)


def _readout0(ctx, x, yf, yb, bonus, g, mods, p, gain_mlp, w1, w2, consts):
    b, t, d = yf.shape
    nt = t // TM
    tile = pl.BlockSpec((1, TM, d), lambda bb, i: (bb, i, 0))
    weights = [p["ln_w"], p["ln_b"], p["wo"], gain_mlp, w1, w2, consts["seg"], consts["segt"]]
    return pl.pallas_call(
        _readout0_kernel,
        grid=(b, nt),
        in_specs=[pl.BlockSpec((1, TM, d), lambda bb, i: (bb, 0, 0)),
                  pl.BlockSpec((1, TM, d), lambda bb, i: (bb, jnp.maximum(i - 1, 0), 0))]
        + [tile] * 4 + [pl.BlockSpec((1, 1, 6, d), lambda bb, i: (bb, jnp.minimum(i, 1), 0, 0))]
        + [_const_spec(w.shape) for w in weights],
        out_specs=tile,
        out_shape=jax.ShapeDtypeStruct((b, t, d), F32),
        compiler_params=_cparams(("parallel", "parallel")),
        name="rwkv_readout_mlp",
    )(ctx, x, yf, yb, bonus, g, mods, *weights)


def _rope(x, cos, sin_signed):
    lane = lax.broadcasted_iota(jnp.int32, (1, 128), 1)
    low = (lane & 16) == 0
    cols = []
    for j in range(x.shape[1] // 128):
        xj = x[:, j * 128:(j + 1) * 128]
        partner = jnp.where(low, pltpu.roll(xj, 128 - 16, 1), pltpu.roll(xj, 16, 1))
        cols.append(xj * cos + partner * sin_signed)
    return jnp.concatenate(cols, axis=1)


def _qkv1_kernel(ctx_tiles, x_ref, mod_ref, gain_ref, wq_ref, wk_ref, wvt_ref, qn_ref, kn_ref, cos_ref, sin_ref,
                 seg_ref, segt_ref, segk_ref, segkt_ref, q_out, k_out, vt_out):
    hb = _norm_mod(x_ref[0], gain_ref[...], mod_ref[0, 0, 0:1, :], mod_ref[0, 0, 1:2, :]).astype(BF16)
    cos = cos_ref[...]
    sin = sin_ref[...]

    q = jnp.dot(hb, wq_ref[...], preferred_element_type=F32)
    q = q * lax.rsqrt(_seg_sum(q * q, seg_ref[...], segt_ref[...], split_input=False) * (1.0 / HEAD)
                      + NORM_EPS) * qn_ref[...]
    q_out[0] = (_rope(q, cos, sin) * (LOG2E * HEAD ** -0.5)).astype(BF16)

    k = jnp.dot(hb, wk_ref[...], preferred_element_type=F32)
    vt = lax.dot_general(wvt_ref[...], hb, _NT, preferred_element_type=F32).astype(BF16)
    k = k * lax.rsqrt(_seg_sum(k * k, segk_ref[...], segkt_ref[...], split_input=False) * (1.0 / HEAD)
                      + NORM_EPS) * kn_ref[...]
    k = _rope(k, cos, sin).astype(BF16)
    ones = jnp.ones((VT_ROWS - HEAD, vt.shape[1]), BF16)
    for gi in range(KV_HEADS):
        k_out[0, gi] = k[:, gi * HEAD:(gi + 1) * HEAD]
        vt_out[0, gi] = jnp.concatenate([vt[gi * HEAD:(gi + 1) * HEAD, :], ones], axis=0)


def _qkv1(x2, mods, gain, wq, wk, wvt, qn, kn, cos, sin, consts, ctx_tiles):
    b, t, d = x2.shape
    nt = t // TM
    tile = pl.BlockSpec((1, TM, d), lambda bb, i: (bb, i, 0))
    tab = pl.BlockSpec((TM, 128), lambda bb, i: (i, 0))
    weights = [gain, wq, wk, wvt, qn, kn]
    segs = [consts["seg"], consts["segt"], consts["segk"], consts["segkt"]]
    return pl.pallas_call(
        functools.partial(_qkv1_kernel, ctx_tiles),
        grid=(b, nt),
        in_specs=[tile, pl.BlockSpec((1, 1, 6, d), lambda bb, i: (bb, jnp.minimum(i, 1), 0, 0))]
        + [_const_spec(w.shape) for w in weights] + [tab, tab] + [_const_spec(w.shape) for w in segs],
        out_specs=[pl.BlockSpec((1, TM, d), lambda bb, i: (bb, jnp.maximum(i - ctx_tiles, 0), 0)),
                   pl.BlockSpec((1, KV_HEADS, TM, HEAD), lambda bb, i: (bb, 0, i, 0)),
                   pl.BlockSpec((1, KV_HEADS, VT_ROWS, TM), lambda bb, i: (bb, 0, 0, i))],
        out_shape=[jax.ShapeDtypeStruct((b, t - ctx_tiles * TM, d), BF16),
                   jax.ShapeDtypeStruct((b, KV_HEADS, t, HEAD), BF16),
                   jax.ShapeDtypeStruct((b, KV_HEADS, VT_ROWS, t), BF16)],
        compiler_params=_cparams(("parallel", "arbitrary")),
        name="attn_qkv",
    )(x2, mods, *weights, cos, sin, *segs)


def _attn_kernel(q_ref, k_ref, vt_ref, o_ref):
    nheads = q_ref.shape[-1] // HEAD
    nkeys = k_ref.shape[2]
    bounds = list(range(0, nkeys, KEY_BLOCK)) + [nkeys]
    nblk = len(bounds) - 1
    q = [q_ref[0, :, j * HEAD:(j + 1) * HEAD] for j in range(nheads)]

    def scores(i):
        kb = k_ref[0, 0, bounds[i]:bounds[i + 1], :]
        return [lax.dot_general(kb, qj, _NT, preferred_element_type=F32) for qj in q]

    m = [None] * nheads
    acc = [None] * nheads
    s_next = scores(0)
    for i in range(nblk):
        s = s_next
        if i + 1 < nblk:
            s_next = scores(i + 1)
        vb = vt_ref[0, 0, :, bounds[i]:bounds[i + 1]]
        for j in range(nheads):
            bmax = jnp.max(s[j], axis=0, keepdims=True)
            m_new = bmax if i == 0 else jnp.maximum(m[j], bmax)
            p = jnp.exp2(s[j] - m_new).astype(BF16)
            pv = jnp.dot(vb, p, preferred_element_type=F32)
            acc[j] = pv if i == 0 else acc[j] * jnp.exp2(m[j] - m_new) + pv
            m[j] = m_new
    outs = [a[:HEAD] / a[HEAD:HEAD + 1] for a in acc]
    o_ref[0] = jnp.concatenate(outs, axis=0).T.astype(BF16)


def _attention(q, k, vt):
    b, s, d = q.shape
    t = k.shape[2]
    gw = d // KV_HEADS
    return pl.pallas_call(
        _attn_kernel,
        grid=(b, KV_HEADS, s // TQ),
        in_specs=[pl.BlockSpec((1, TQ, gw), lambda bb, gi, i: (bb, i, gi)),
                  pl.BlockSpec((1, 1, t, HEAD), lambda bb, gi, i: (bb, gi, 0, 0)),
                  pl.BlockSpec((1, 1, VT_ROWS, t), lambda bb, gi, i: (bb, gi, 0, 0))],
        out_specs=pl.BlockSpec((1, TQ, gw), lambda bb, gi, i: (bb, i, gi)),
        out_shape=jax.ShapeDtypeStruct((b, s, d), BF16),
        compiler_params=_cparams(("parallel", "parallel", "parallel")),
        name="gqa_attention",
    )(q, k, vt)


def _out1_kernel(x_ref, a_ref, mod_ref, wo_ref, gain_ref, w1_ref, w2_ref, fin_ref, o_ref):
    o = jnp.dot(a_ref[0], wo_ref[...], preferred_element_type=F32)
    x1 = x_ref[0] + mod_ref[0, 0, 2:3, :] * o
    h2 = _norm_mod(x1, gain_ref[...], mod_ref[0, 0, 3:4, :], mod_ref[0, 0, 4:5, :])
    x2 = x1 + mod_ref[0, 0, 5:6, :] * _mlp(h2, w1_ref, w2_ref)
    ms = jnp.mean(x2 * x2, axis=-1, keepdims=True)
    o_ref[0] = x2 * lax.rsqrt(ms + NORM_EPS) * fin_ref[...]


def _out1(x2, att, mods, wo, gain_mlp, w1, w2, fin, ctx_tiles):
    b, s, d = att.shape
    weights = [wo, gain_mlp, w1, w2, fin]
    return pl.pallas_call(
        _out1_kernel,
        grid=(b, s // TM),
        in_specs=[pl.BlockSpec((1, TM, d), lambda bb, i: (bb, i + ctx_tiles, 0)),
                  pl.BlockSpec((1, TM, d), lambda bb, i: (bb, i, 0)),
                  pl.BlockSpec((1, 1, 6, d), lambda bb, i: (bb, 1, 0, 0))]
        + [_const_spec(w.shape) for w in weights],
        out_specs=pl.BlockSpec((1, TM, d), lambda bb, i: (bb, i, 0)),
        out_shape=jax.ShapeDtypeStruct((b, s, d), F32),
        compiler_params=_cparams(("parallel", "parallel")),
        name="attn_out_mlp_final",
    )(x2, att, mods, *weights)


def _block_diag2(a, b):
    za = jnp.zeros((a.shape[0], b.shape[1]), a.dtype)
    zb = jnp.zeros((b.shape[0], a.shape[1]), b.dtype)
    return jnp.concatenate([jnp.concatenate([a, za], 1), jnp.concatenate([zb, b], 1)], 0)


def _constants(d):
    lane = jnp.arange(d)[:, None] // HEAD
    seg = (lane == jnp.arange(SEG_W)[None, :]).astype(BF16)
    nk = KV_HEADS * HEAD
    segk = seg[:nk]
    t = jnp.arange(TM)
    same = (t[:, None] // CHUNK) == (t[None, :] // CHUNK)
    tri_f = (same & (t[None, :] <= t[:, None])).astype(BF16)
    tri_b = (same & (t[None, :] >= t[:, None])).astype(BF16)
    return dict(seg=seg, segt=seg.T, segk=segk, segkt=segk.T, tri_f=tri_f, tri_b=tri_b)


def _rope_tables(s, c_len):
    tok = jnp.arange(s)
    row = (tok // GRID_W).astype(F32)
    col = (tok % GRID_W).astype(F32)
    half = HEAD // 2
    freqs = ROPE_THETA ** (-jnp.arange(0, half, 2, dtype=F32) / half)
    ang_r = row[:, None] * freqs
    ang_c = col[:, None] * freqs
    cos = jnp.concatenate([jnp.cos(ang_r)] * 2 + [jnp.cos(ang_c)] * 2, axis=1)
    sin = jnp.concatenate([-jnp.sin(ang_r), jnp.sin(ang_r), -jnp.sin(ang_c), jnp.sin(ang_c)], axis=1)
    cos = jnp.concatenate([jnp.ones((c_len, HEAD), F32), cos], axis=0)
    sin = jnp.concatenate([jnp.zeros((c_len, HEAD), F32), sin], axis=0)
    return jnp.tile(cos, (1, 2)), jnp.tile(sin, (1, 2))


def kernel(x, c, ctx, c_ctx, w_mod, b_mod, norm_mix, norm_mlp, mlp_w1, mlp_w2, rwkv_mu, rwkv_wr, rwkv_wk, rwkv_wv,
           rwkv_wo, rwkv_w0, rwkv_w1, rwkv_w2, rwkv_a0, rwkv_a1, rwkv_a2, rwkv_g1, rwkv_g2, rwkv_k_k, rwkv_k_a,
           rwkv_r_k, rwkv_ln_w, rwkv_ln_b, attn_wqkv, attn_q_norm, attn_k_norm, attn_wo, final_norm):
    b, s, d = x.shape
    c_len = ctx.shape[1]
    assert d % (4 * 128) == 0 and d // HEAD <= SEG_W
    assert s % TQ == 0 and TQ % TM == 0 and TM % GRID_W == 0 and TM % CHUNK == 0
    assert c_len == TM, "the context must fill exactly one token tile"
    assert w_mod.shape[0] == 2 and rwkv_mu.shape[0] == 1 and attn_wqkv.shape[0] == 1
    ctx_tiles = c_len // TM
    row = lambda a: a.reshape(1, -1).astype(F32)
    bf = lambda a: a.astype(BF16)
    consts = _constants(d)

    rows = -(-(b + 1) // 8) * 8
    cvec = jnp.zeros((rows, d), F32).at[:b].set(c).at[b].set(c_ctx)
    m = _modulation(cvec, w_mod, b_mod)
    m = m.reshape(2, rows, 6, d)
    mods = [jnp.stack([jnp.broadcast_to(m[l, b][None], (b, 6, d)), m[l, :b]], axis=1) for l in range(2)]

    p0 = dict(
        mu=rwkv_mu[0], wr=bf(rwkv_wr[0]), wk=bf(rwkv_wk[0]), wv=bf(rwkv_wv[0]), wo=bf(rwkv_wo[0]),
        g1=bf(rwkv_g1[0]), g2=bf(rwkv_g2[0]),
        w1=bf(jnp.concatenate([rwkv_w1[0, 0], rwkv_w1[0, 1]], axis=1)),
        w2=bf(_block_diag2(rwkv_w2[0, 0], rwkv_w2[0, 1])),
        w0=rwkv_w0[0].reshape(1, -1),
        a1=bf(jnp.concatenate([rwkv_a1[0, 0], rwkv_a1[0, 1]], axis=1)),
        a2=bf(_block_diag2(rwkv_a2[0, 0], rwkv_a2[0, 1])),
        a0=rwkv_a0[0].reshape(1, -1),
        k_k=row(rwkv_k_k[0]), k_a=row(rwkv_k_a[0]), r_k=row(rwkv_r_k[0]),
        ln_w=row(rwkv_ln_w[0]), ln_b=row(rwkv_ln_b[0]),
    )
    (v, g, bonus, a_f, r_f, b_f, k_f, a_b, r_b, b_b, k_b, dl_f, dl_b) = _proj0(
        ctx, x, mods[0], row(norm_mix[0]), p0, consts)
    w_f, u_f = _wkv_intra(False, a_f, b_f, k_f, v)
    w_b, u_b = _wkv_intra(True, a_b, b_b, k_b, v)
    y_f, y_b = _wkv_scan(ctx_tiles, (w_f, u_f, r_f, b_f, k_f, v, dl_f), (w_b, u_b, r_b, b_b, k_b, v, dl_b))
    x1 = _readout0(ctx, x, y_f, y_b, bonus, g, mods[0], p0, row(norm_mlp[0]), bf(mlp_w1[0]), bf(mlp_w2[0]),
                   consts)

    cos, sin = _rope_tables(s, c_len)
    qn = jnp.tile(attn_q_norm[0], d // HEAD).reshape(1, -1)
    kn = jnp.tile(attn_k_norm[0], KV_HEADS).reshape(1, -1)
    nqk = d + KV_HEADS * HEAD
    wqkv = attn_wqkv[0]
    q, k, vt = _qkv1(x1, mods[1], row(norm_mix[1]), bf(wqkv[:, :d]), bf(wqkv[:, d:nqk]), bf(wqkv[:, nqk:].T),
                     qn, kn, cos, sin, consts, ctx_tiles)
    att = _attention(q, k, vt)
    return _out1(x1, att, mods[1], bf(attn_wo[0]), row(norm_mlp[1]), bf(mlp_w1[1]), bf(mlp_w2[1]),
                 row(final_norm), ctx_tiles)
```

```python
import functools
import math

import jax
import jax.numpy as jnp
from jax import lax
from jax.experimental import pallas as pl
from jax.experimental.pallas import tpu as pltpu

F32 = jnp.float32
BF16 = jnp.bfloat16

HEAD = 64
GRID_W = 64
KV_HEADS = 4
NORM_EPS = 1e-6
GN_EPS = 64e-5
ROPE_THETA = 10000.0
CHUNK = 64
TM = 256
SEG_W = 128
VMEM_LIMIT = 56 * 1024 * 1024
EXP_NEG_HALF = math.exp(-0.5)
LOG2E = math.log2(math.e)
KEY_BLOCK = 256
TQ = 256
VT_ROWS = 2 * HEAD
INTRA_CHUNKS_PER_ITER = 4
MLP_FF_CHUNK = 1024


def _bdot(a, b):
    return jnp.dot(a.astype(BF16), b.astype(BF16), preferred_element_type=F32)


def _split_hi_lo(x):
    hi = x.astype(BF16)
    lo = (x - hi.astype(F32)).astype(BF16)
    return hi, lo


def _seg_sum(x, seg, segt, split_input=True):
    if split_input:
        hi, lo = _split_hi_lo(x)
        s = jnp.dot(hi, seg, preferred_element_type=F32) + jnp.dot(lo, seg, preferred_element_type=F32)
    else:
        s = jnp.dot(x.astype(BF16), seg, preferred_element_type=F32)
    s_hi, s_lo = _split_hi_lo(s)
    return jnp.dot(s_hi, segt, preferred_element_type=F32) + jnp.dot(s_lo, segt, preferred_element_type=F32)


def _sigmoid(x):
    return 0.5 * jnp.tanh(0.5 * x) + 0.5


def _norm_mod(x, gain, shift, scale):
    ms = jnp.mean(x * x, axis=-1, keepdims=True)
    return x * lax.rsqrt(ms + NORM_EPS) * (gain * (1.0 + scale)) + shift


def _cparams(sem):
    return pltpu.CompilerParams(dimension_semantics=sem, vmem_limit_bytes=VMEM_LIMIT)


def _const_spec(shape):
    nd = len(shape)
    return pl.BlockSpec(shape, lambda *_: (0,) * nd)


def _mod_kernel(c_ref, w_ref, b_ref, o_ref):
    c = c_ref[...]
    s = c * jax.nn.sigmoid(c)
    s_hi, s_lo = _split_hi_lo(s)
    w = w_ref[0]
    w_hi, w_lo = _split_hi_lo(w)
    acc = jnp.dot(s_hi, w_hi, preferred_element_type=F32)
    acc += jnp.dot(s_hi, w_lo, preferred_element_type=F32)
    acc += jnp.dot(s_lo, w_hi, preferred_element_type=F32)
    o_ref[0] = acc + b_ref[0]


def _modulation(cvec, w_mod, b_mod):
    depth, d, n = w_mod.shape
    rows = cvec.shape[0]
    tn = 512
    return pl.pallas_call(
        _mod_kernel,
        grid=(depth, n // tn),
        in_specs=[
            pl.BlockSpec((rows, d), lambda l, j: (0, 0)),
            pl.BlockSpec((1, d, tn), lambda l, j: (l, 0, j)),
            pl.BlockSpec((1, 1, tn), lambda l, j: (l, 0, j)),
        ],
        out_specs=pl.BlockSpec((1, rows, tn), lambda l, j: (l, 0, j)),
        out_shape=jax.ShapeDtypeStruct((depth, rows, n), F32),
        compiler_params=_cparams(("parallel", "parallel")),
        name="adaln_modulation",
    )(cvec, w_mod, b_mod.reshape(depth, 1, n))


def _proj0_kernel(nt, cx_ref, xc_ref, xp_ref, xn_ref, mod_ref, gain_ref, mu_ref, wr_ref, wk_ref, wv_ref, g1_ref,
                  g2_ref, w1_ref, w2_ref, w0_ref, a1_ref, a2_ref, a0_ref, kk_ref, ka_ref, rk_ref, seg_ref, segt_ref,
                  trif_ref, trib_ref,
                  v_out, g_out, bonus_out, af_out, rf_out, bf_out, kf_out, ab_out, rb_out, bb_out, kb_out,
                  dlf_out, dlb_out, hs_ref):
    i = pl.program_id(1)
    d = xc_ref.shape[-1]
    q = d // 4
    gain = gain_ref[...]
    shift = mod_ref[0, 0, 0:1, :]
    scale = mod_ref[0, 0, 1:2, :]
    h = _norm_mod(jnp.where(i == 0, cx_ref[0], xc_ref[0]), gain, shift, scale)
    t_idx = lax.broadcasted_iota(jnp.int32, (TM, 1), 0)

    @pl.when(i == 0)
    def _():
        prev = jnp.where(t_idx == 0, 0.0, pltpu.roll(h[:, :2 * q], 1, 0))
        nxt = jnp.where(t_idx == TM - 1, 0.0, pltpu.roll(h[:, 2 * q:], TM - 1, 0))
        hs_ref[:, :2 * q] = prev
        hs_ref[:, 2 * q:] = nxt

    @pl.when(i > 0)
    def _():
        col = t_idx % GRID_W
        left = jnp.where(col == 0, 0.0, pltpu.roll(h[:, :q], 1, 0))
        right = jnp.where(col == GRID_W - 1, 0.0, pltpu.roll(h[:, q:2 * q], TM - 1, 0))
        hp = _norm_mod(xp_ref[0], gain, shift, scale)[:, 2 * q:3 * q]
        hn = _norm_mod(xn_ref[0], gain, shift, scale)[:, 3 * q:]
        hp = jnp.where(i > 1, hp, 0.0)
        hn = jnp.where(i < nt - 1, hn, 0.0)
        hs_ref[:, :q] = left
        hs_ref[:, q:2 * q] = right
        hs_ref[:, 2 * q:3 * q] = jnp.concatenate([hp, h[:TM - GRID_W, 2 * q:3 * q]], axis=0)
        hs_ref[:, 3 * q:] = jnp.concatenate([h[GRID_W:, 3 * q:], hn], axis=0)

    mub = mu_ref[...].astype(BF16)
    seg = seg_ref[...]
    segt = segt_ref[...]
    hb = h.astype(BF16)
    xxb = (hs_ref[...] - h).astype(BF16)
    mix = lambda j: hb + xxb * mub[j:j + 1, :]

    wl = _bdot(jnp.tanh(jnp.dot(mix(1), w1_ref[...], preferred_element_type=F32)), w2_ref[...]) + w0_ref[...]
    al = _bdot(jnp.dot(mix(4), a1_ref[...], preferred_element_type=F32), a2_ref[...]) + a0_ref[...]
    e = EXP_NEG_HALF * _sigmoid(wl)
    e_hi, e_lo = _split_hi_lo(e)
    dirs = []
    for dr, (tri_ref, dl_o) in enumerate(((trif_ref, dlf_out), (trib_ref, dlb_out))):
        tri = tri_ref[...]
        cols = slice(dr * d, (dr + 1) * d)
        c = (jnp.dot(tri, e_hi[:, cols], preferred_element_type=F32)
             + jnp.dot(tri, e_lo[:, cols], preferred_element_type=F32))
        grow = jnp.exp(c)
        shrink = jnp.exp(-c)
        last = CHUNK - 1 if dr == 0 else 0
        dl_o[0, 0] = jnp.concatenate(
            [shrink[n * CHUNK + last:n * CHUNK + last + 1, :] for n in range(TM // CHUNK)], axis=0)
        dirs.append((_sigmoid(al[:, cols]), grow, shrink, jnp.exp(e[:, cols] - c)))

    g = _bdot(_sigmoid(jnp.dot(mix(5), g1_ref[...], preferred_element_type=F32)), g2_ref[...])
    g_out[0] = g.astype(BF16)
    k = jnp.dot(mix(2), wk_ref[...], preferred_element_type=F32)
    r = jnp.dot(mix(0), wr_ref[...], preferred_element_type=F32)
    v = jnp.dot(mix(3), wv_ref[...], preferred_element_type=F32)
    v_out[0] = v.astype(BF16)

    kk = k * kk_ref[...]
    kk = kk * lax.rsqrt(_seg_sum(kk * kk, seg, segt, split_input=False) + 1e-12)
    ksum = None
    outs = ((af_out, rf_out, bf_out, kf_out), (ab_out, rb_out, bb_out, kb_out))
    for (a_o, r_o, b_o, k_o), (a_lr, grow, shrink, lag) in zip(outs, dirs):
        k_dir = k * (1.0 + (a_lr - 1.0) * ka_ref[...])
        ksum = k_dir if ksum is None else ksum + k_dir
        a_o[0] = (-kk * lag).astype(BF16)
        r_o[0] = (r * shrink).astype(BF16)
        b_o[0] = (kk * a_lr * grow).astype(BF16)
        k_o[0] = (k_dir * grow).astype(BF16)
    bonus_out[0] = (_seg_sum(r * ksum * rk_ref[...], seg, segt) * v).astype(BF16)


def _proj0(ctx, x, mods, gain, p, consts):
    b, s, d = x.shape
    t = s + ctx.shape[1]
    nt = t // TM
    hb = TM // GRID_W
    nhb = s // GRID_W
    tok = lambda bb, i: (bb, i, 0)
    big = jax.ShapeDtypeStruct((b, t, d), BF16)
    dl = jax.ShapeDtypeStruct((b, nt, TM // CHUNK, d), F32)
    tile_spec = pl.BlockSpec((1, TM, d), tok)
    dl_spec = pl.BlockSpec((1, 1, TM // CHUNK, d), lambda bb, i: (bb, i, 0, 0))
    weights = [p["mu"], p["wr"], p["wk"], p["wv"], p["g1"], p["g2"], p["w1"], p["w2"], p["w0"],
               p["a1"], p["a2"], p["a0"], p["k_k"], p["k_a"], p["r_k"],
               consts["seg"], consts["segt"], consts["tri_f"], consts["tri_b"]]
    return pl.pallas_call(
        functools.partial(_proj0_kernel, nt),
        grid=(b, nt),
        in_specs=[
            pl.BlockSpec((1, TM, d), lambda bb, i: (bb, 0, 0)),
            pl.BlockSpec((1, TM, d), lambda bb, i: (bb, jnp.maximum(i - 1, 0), 0)),
            pl.BlockSpec((1, GRID_W, d), lambda bb, i: (bb, jnp.maximum((i - 1) * hb - 1, 0), 0)),
            pl.BlockSpec((1, GRID_W, d), lambda bb, i: (bb, jnp.minimum(i * hb, nhb - 1), 0)),
            pl.BlockSpec((1, 1, 6, d), lambda bb, i: (bb, jnp.minimum(i, 1), 0, 0)),
            _const_spec(gain.shape),
        ] + [_const_spec(w.shape) for w in weights],
        out_specs=[tile_spec] * 11 + [dl_spec] * 2,
        out_shape=[big] * 11 + [dl] * 2,
        scratch_shapes=[pltpu.VMEM((TM, d), F32)],
        compiler_params=_cparams(("parallel", "parallel")),
        name="rwkv_project",
    )(ctx, x, x, x, mods, gain, *weights)


_NT = (((1,), (1,)), ((), ()))
_TN = (((0,), (0,)), ((), ()))
PAIR = 2 * HEAD


def _pair_masks(rows):
    lane = lax.broadcasted_iota(jnp.int32, (rows, PAIR), 1)
    return lane < HEAD, lane >= HEAD


def _tri_mask(reverse, inclusive):
    row = lax.broadcasted_iota(jnp.int32, (CHUNK, PAIR), 0)
    col = lax.broadcasted_iota(jnp.int32, (CHUNK, PAIR), 1) % HEAD
    if inclusive:
        return (col >= row) if reverse else (col <= row)
    return (col > row) if reverse else (col < row)


def _wkv_intra_kernel(reverse, a_ref, b_ref, k_ref, v_ref, w_out, u_out):
    npairs = a_ref.shape[-1] // PAIR
    h0, h1 = _pair_masks(CHUNK)
    lane2 = lax.broadcasted_iota(jnp.int32, (CHUNK, 2 * PAIR), 1)
    row2 = lax.broadcasted_iota(jnp.int32, (CHUNK, 2 * PAIR), 0)
    first = (lane2 // HEAD) % 2 == 0
    left = lane2 < PAIR
    col2 = lane2 % HEAD
    strict = (col2 > row2) if reverse else (col2 < row2)
    eye_right = ((col2 == row2) & (lane2 >= PAIR)).astype(F32)
    stack2 = lambda t: jnp.concatenate([jnp.where(h0, t, 0), jnp.where(h1, t, 0)], axis=0)

    per_iter = INTRA_CHUNKS_PER_ITER

    def chunk(ci, carry):
        rows = [pl.ds(pl.multiple_of((ci * per_iter + n) * CHUNK, CHUNK), CHUNK) for n in range(per_iter)]
        psl = lambda p: slice(p * PAIR, (p + 1) * PAIR)
        slabs = [(n, p) for n in range(per_iter) for p in range(npairs)]
        a = [a_ref[0, rows[n], psl(p)] for n, p in slabs]
        v = [v_ref[0, rows[n], psl(p)] for n, p in slabs]
        bk = [jnp.concatenate([stack2(b_ref[0, rows[n], psl(p)]), stack2(k_ref[0, rows[n], psl(p)])], axis=0)
              for n, p in slabs]
        sc = [jnp.where(strict, lax.dot_general(a[g], bk[g], _NT, preferred_element_type=F32), 0.0)
              for g in range(len(slabs))]
        x = [jnp.where(left, t, eye_right) for t in sc]
        for _ in range(6):
            xb = [t.astype(BF16) for t in x]
            wts = [jnp.concatenate([jnp.where(first, t, 0), jnp.where(first, 0, t)], axis=0) for t in xb]
            res = [jnp.dot(t[:, :PAIR], w, preferred_element_type=F32) for t, w in zip(xb, wts)]
            x = [r + jnp.where(left, 0.0, t) for r, t in zip(res, x)]
        tinv = [t[:, PAIR:].astype(BF16) for t in x]
        akv = [jnp.dot(sc[g][:, PAIR:].astype(BF16), stack2(v[g]), preferred_element_type=F32).astype(BF16)
               for g in range(len(slabs))]
        wu = [jnp.dot(tinv[g], jnp.concatenate([stack2(a[g]), stack2(akv[g])], axis=1),
                      preferred_element_type=F32) for g in range(len(slabs))]
        for n in range(per_iter):
            mine = wu[n * npairs:(n + 1) * npairs]
            w_out[0, rows[n], :] = jnp.concatenate([t[:, :PAIR] for t in mine], axis=1).astype(BF16)
            u_out[0, rows[n], :] = jnp.concatenate([t[:, PAIR:] for t in mine], axis=1).astype(BF16)
        return carry

    lax.fori_loop(0, a_ref.shape[1] // (CHUNK * per_iter), chunk, 0)


def _wkv_intra(reverse, a, bm, k, v):
    b, t, d = a.shape
    tile = pl.BlockSpec((1, TM, d), lambda bb, i: (bb, i, 0))
    return pl.pallas_call(
        functools.partial(_wkv_intra_kernel, reverse),
        grid=(b, t // TM),
        in_specs=[tile] * 4,
        out_specs=[tile, tile],
        out_shape=[jax.ShapeDtypeStruct((b, t, d), BF16)] * 2,
        compiler_params=_cparams(("parallel", "parallel")),
        name="wkv_intra_bwd" if reverse else "wkv_intra_fwd",
    )(a, bm, k, v)


def _wkv_scan_kernel(wf, uf, rf, bf, kf, vf, dlf, wb, ub, rb, bb, kb, vb, dlb, yf_ref, yb_ref, z_ref):
    dirs = ((False, wf, uf, rf, bf, kf, vf, dlf, yf_ref), (True, wb, ub, rb, bb, kb, vb, dlb, yb_ref))
    npairs = wf.shape[-1] // PAIR
    nchunks = wf.shape[1] // CHUNK
    slabs = [(dd, p) for dd in range(2) for p in range(npairs)]
    h0, h1 = _pair_masks(CHUNK)
    stack2 = lambda t: jnp.concatenate([jnp.where(h0, t, 0), jnp.where(h1, t, 0)], axis=0)
    lane2 = lax.broadcasted_iota(jnp.int32, (CHUNK, 2 * PAIR), 1) % HEAD
    row2 = lax.broadcasted_iota(jnp.int32, (CHUNK, 2 * PAIR), 0)
    incl = (lane2 <= row2, lane2 >= row2)
    brow = lax.broadcasted_iota(jnp.int32, (PAIR, PAIR), 0) // HEAD
    bcol = lax.broadcasted_iota(jnp.int32, (PAIR, PAIR), 1) // HEAD
    same_head = brow == bcol
    psl = lambda p: slice(p * PAIR, (p + 1) * PAIR)

    @pl.when(pl.program_id(1) == 0)
    def _():
        z_ref[...] = jnp.zeros(z_ref.shape, F32)

    z = [z_ref[dd, p] for dd, p in slabs]
    for n in range(nchunks):
        rows = [slice((nchunks - 1 - n) * CHUNK, (nchunks - n) * CHUNK) if dirs[dd][0]
                else slice(n * CHUNK, (n + 1) * CHUNK) for dd, _ in slabs]
        cidx = [(nchunks - 1 - n) if dirs[dd][0] else n for dd, _ in slabs]
        ld = lambda pos, g: dirs[slabs[g][0]][pos][0, rows[g], psl(slabs[g][1])]
        ng = len(slabs)
        r = [ld(3, g) for g in range(ng)]
        v = [ld(6, g) for g in range(ng)]
        bm = [ld(4, g) for g in range(ng)]
        km = [ld(5, g) for g in range(ng)]
        bk = [jnp.concatenate([bm[g], km[g]], axis=0) for g in range(ng)]
        bk4 = [jnp.concatenate([stack2(bm[g]), stack2(km[g])], axis=0) for g in range(ng)]
        wr = [jnp.concatenate([ld(1, g), r[g]], axis=0) for g in range(ng)]
        sr = [jnp.where(incl[slabs[g][0]], lax.dot_general(r[g], bk4[g], _NT, preferred_element_type=F32),
                        0.0).astype(BF16) for g in range(ng)]
        ws = [lax.dot_general(wr[g], z[g].astype(BF16), _NT, preferred_element_type=F32) for g in range(ng)]
        ub = [(ws[g][:CHUNK] + ld(2, g)).astype(BF16) for g in range(ng)]
        uv = [jnp.concatenate([ub[g], v[g]], axis=0) for g in range(ng)]
        uv4 = [jnp.concatenate([stack2(ub[g]), stack2(v[g])], axis=0) for g in range(ng)]
        ys = [ws[g][CHUNK:] + jnp.dot(sr[g], uv4[g], preferred_element_type=F32) for g in range(ng)]
        upd = [lax.dot_general(uv[g], bk[g], _TN, preferred_element_type=F32) for g in range(ng)]
        z = [(z[g] + jnp.where(same_head, upd[g], 0.0))
             * dirs[slabs[g][0]][7][0, 0, cidx[g]:cidx[g] + 1, psl(slabs[g][1])] for g in range(ng)]
        for dd in range(2):
            dirs[dd][8][0, rows[dd * npairs], :] = jnp.concatenate(
                ys[dd * npairs:(dd + 1) * npairs], axis=1).astype(BF16)
    for g, (dd, p) in enumerate(slabs):
        z_ref[dd, p] = z[g]


def _wkv_scan(ctx_tiles, fwd, bwd):
    b, t, d = fwd[0].shape
    nt = t // TM
    blk_b = lambda j: jnp.where(j < ctx_tiles, ctx_tiles - 1 - j, nt - 1 - j + ctx_tiles)
    specs = []
    for blk in (lambda j: j, blk_b):
        tile = pl.BlockSpec((1, TM, d), lambda bb, j, blk=blk: (bb, blk(j), 0))
        dl_spec = pl.BlockSpec((1, 1, TM // CHUNK, d), lambda bb, j, blk=blk: (bb, blk(j), 0, 0))
        specs.append((tile, dl_spec))
    return pl.pallas_call(
        _wkv_scan_kernel,
        grid=(b, nt),
        in_specs=[specs[0][0]] * 6 + [specs[0][1]] + [specs[1][0]] * 6 + [specs[1][1]],
        out_specs=[specs[0][0], specs[1][0]],
        out_shape=[jax.ShapeDtypeStruct((b, t, d), BF16)] * 2,
        scratch_shapes=[pltpu.VMEM((2, d // PAIR, PAIR, PAIR), F32)],
        compiler_params=_cparams(("parallel", "arbitrary")),
        name="wkv_scan",
    )(*fwd, *bwd)


def _mlp(h, w1_ref, w2_ref):
    hb = h.astype(BF16)
    dff = w1_ref.shape[1]
    fc = MLP_FF_CHUNK
    acc = None
    for j in range(dff // fc):
        a = jnp.dot(hb, w1_ref[:, j * fc:(j + 1) * fc], preferred_element_type=F32)
        a = jnp.square(jnp.maximum(a, 0.0)).astype(BF16)
        part = jnp.dot(a, w2_ref[j * fc:(j + 1) * fc, :], preferred_element_type=F32)
        acc = part if acc is None else acc + part
    return acc


def _readout0_kernel(cx_ref, x_ref, yf_ref, yb_ref, bonus_ref, g_ref, mod_ref, lnw_ref, lnb_ref, wo_ref, gain_ref,
                     w1_ref, w2_ref, seg_ref, segt_ref, o_ref):
    x_in = jnp.where(pl.program_id(1) == 0, cx_ref[0], x_ref[0])
    seg = seg_ref[...]
    segt = segt_ref[...]
    y = yf_ref[0].astype(F32) + yb_ref[0].astype(F32)
    mean = _seg_sum(y, seg, segt, split_input=False) * (1.0 / HEAD)
    dev = y - mean
    var = _seg_sum(dev * dev, seg, segt, split_input=False) * (1.0 / HEAD)
    yn = dev * lax.rsqrt(var + GN_EPS) * lnw_ref[...] + lnb_ref[...]
    mixed = (yn + bonus_ref[0].astype(F32)) * g_ref[0].astype(F32)
    o = _bdot(mixed, wo_ref[...])
    x1 = x_in + mod_ref[0, 0, 2:3, :] * o
    h2 = _norm_mod(x1, gain_ref[...], mod_ref[0, 0, 3:4, :], mod_ref[0, 0, 4:5, :])
    o_ref[0] = x1 + mod_ref[0, 0, 5:6, :] * _mlp(h2, w1_ref, w2_ref)


def _readout0(ctx, x, yf, yb, bonus, g, mods, p, gain_mlp, w1, w2, consts):
    b, t, d = yf.shape
    nt = t // TM
    tile = pl.BlockSpec((1, TM, d), lambda bb, i: (bb, i, 0))
    weights = [p["ln_w"], p["ln_b"], p["wo"], gain_mlp, w1, w2, consts["seg"], consts["segt"]]
    return pl.pallas_call(
        _readout0_kernel,
        grid=(b, nt),
        in_specs=[pl.BlockSpec((1, TM, d), lambda bb, i: (bb, 0, 0)),
                  pl.BlockSpec((1, TM, d), lambda bb, i: (bb, jnp.maximum(i - 1, 0), 0))]
        + [tile] * 4 + [pl.BlockSpec((1, 1, 6, d), lambda bb, i: (bb, jnp.minimum(i, 1), 0, 0))]
        + [_const_spec(w.shape) for w in weights],
        out_specs=tile,
        out_shape=jax.ShapeDtypeStruct((b, t, d), F32),
        compiler_params=_cparams(("parallel", "parallel")),
        name="rwkv_readout_mlp",
    )(ctx, x, yf, yb, bonus, g, mods, *weights)


def _rope(x, cos, sin_signed):
    lane = lax.broadcasted_iota(jnp.int32, (1, 128), 1)
    low = (lane & 16) == 0
    cols = []
    for j in range(x.shape[1] // 128):
        xj = x[:, j * 128:(j + 1) * 128]
        partner = jnp.where(low, pltpu.roll(xj, 128 - 16, 1), pltpu.roll(xj, 16, 1))
        cols.append(xj * cos + partner * sin_signed)
    return jnp.concatenate(cols, axis=1)


def _qkv1_kernel(ctx_tiles, x_ref, mod_ref, gain_ref, wq_ref, wk_ref, wvt_ref, qn_ref, kn_ref, cos_ref, sin_ref,
                 seg_ref, segt_ref, segk_ref, segkt_ref, q_out, k_out, vt_out):
    hb = _norm_mod(x_ref[0], gain_ref[...], mod_ref[0, 0, 0:1, :], mod_ref[0, 0, 1:2, :]).astype(BF16)
    cos = cos_ref[...]
    sin = sin_ref[...]

    q = jnp.dot(hb, wq_ref[...], preferred_element_type=F32)
    q = q * lax.rsqrt(_seg_sum(q * q, seg_ref[...], segt_ref[...], split_input=False) * (1.0 / HEAD)
                      + NORM_EPS) * qn_ref[...]
    q_out[0] = (_rope(q, cos, sin) * (LOG2E * HEAD ** -0.5)).astype(BF16)

    k = jnp.dot(hb, wk_ref[...], preferred_element_type=F32)
    vt = lax.dot_general(wvt_ref[...], hb, _NT, preferred_element_type=F32).astype(BF16)
    k = k * lax.rsqrt(_seg_sum(k * k, segk_ref[...], segkt_ref[...], split_input=False) * (1.0 / HEAD)
                      + NORM_EPS) * kn_ref[...]
    k = _rope(k, cos, sin).astype(BF16)
    ones = jnp.ones((VT_ROWS - HEAD, vt.shape[1]), BF16)
    for gi in range(KV_HEADS):
        k_out[0, gi] = k[:, gi * HEAD:(gi + 1) * HEAD]
        vt_out[0, gi] = jnp.concatenate([vt[gi * HEAD:(gi + 1) * HEAD, :], ones], axis=0)


def _qkv1(x2, mods, gain, wq, wk, wvt, qn, kn, cos, sin, consts, ctx_tiles):
    b, t, d = x2.shape
    nt = t // TM
    tile = pl.BlockSpec((1, TM, d), lambda bb, i: (bb, i, 0))
    tab = pl.BlockSpec((TM, 128), lambda bb, i: (i, 0))
    weights = [gain, wq, wk, wvt, qn, kn]
    segs = [consts["seg"], consts["segt"], consts["segk"], consts["segkt"]]
    return pl.pallas_call(
        functools.partial(_qkv1_kernel, ctx_tiles),
        grid=(b, nt),
        in_specs=[tile, pl.BlockSpec((1, 1, 6, d), lambda bb, i: (bb, jnp.minimum(i, 1), 0, 0))]
        + [_const_spec(w.shape) for w in weights] + [tab, tab] + [_const_spec(w.shape) for w in segs],
        out_specs=[pl.BlockSpec((1, TM, d), lambda bb, i: (bb, jnp.maximum(i - ctx_tiles, 0), 0)),
                   pl.BlockSpec((1, KV_HEADS, TM, HEAD), lambda bb, i: (bb, 0, i, 0)),
                   pl.BlockSpec((1, KV_HEADS, VT_ROWS, TM), lambda bb, i: (bb, 0, 0, i))],
        out_shape=[jax.ShapeDtypeStruct((b, t - ctx_tiles * TM, d), BF16),
                   jax.ShapeDtypeStruct((b, KV_HEADS, t, HEAD), BF16),
                   jax.ShapeDtypeStruct((b, KV_HEADS, VT_ROWS, t), BF16)],
        compiler_params=_cparams(("parallel", "arbitrary")),
        name="attn_qkv",
    )(x2, mods, *weights, cos, sin, *segs)


def _attn_kernel(q_ref, k_ref, vt_ref, o_ref):
    nheads = q_ref.shape[-1] // HEAD
    nkeys = k_ref.shape[2]
    bounds = list(range(0, nkeys, KEY_BLOCK)) + [nkeys]
    nblk = len(bounds) - 1
    q = [q_ref[0, :, j * HEAD:(j + 1) * HEAD] for j in range(nheads)]

    def scores(i):
        kb = k_ref[0, 0, bounds[i]:bounds[i + 1], :]
        return [lax.dot_general(kb, qj, _NT, preferred_element_type=F32) for qj in q]

    m = [None] * nheads
    acc = [None] * nheads
    s_next = scores(0)
    for i in range(nblk):
        s = s_next
        if i + 1 < nblk:
            s_next = scores(i + 1)
        vb = vt_ref[0, 0, :, bounds[i]:bounds[i + 1]]
        for j in range(nheads):
            bmax = jnp.max(s[j], axis=0, keepdims=True)
            m_new = bmax if i == 0 else jnp.maximum(m[j], bmax)
            p = jnp.exp2(s[j] - m_new).astype(BF16)
            pv = jnp.dot(vb, p, preferred_element_type=F32)
            acc[j] = pv if i == 0 else acc[j] * jnp.exp2(m[j] - m_new) + pv
            m[j] = m_new
    outs = [a[:HEAD] / a[HEAD:HEAD + 1] for a in acc]
    o_ref[0] = jnp.concatenate(outs, axis=0).T.astype(BF16)


def _attention(q, k, vt):
    b, s, d = q.shape
    t = k.shape[2]
    gw = d // KV_HEADS
    return pl.pallas_call(
        _attn_kernel,
        grid=(b, KV_HEADS, s // TQ),
        in_specs=[pl.BlockSpec((1, TQ, gw), lambda bb, gi, i: (bb, i, gi)),
                  pl.BlockSpec((1, 1, t, HEAD), lambda bb, gi, i: (bb, gi, 0, 0)),
                  pl.BlockSpec((1, 1, VT_ROWS, t), lambda bb, gi, i: (bb, gi, 0, 0))],
        out_specs=pl.BlockSpec((1, TQ, gw), lambda bb, gi, i: (bb, i, gi)),
        out_shape=jax.ShapeDtypeStruct((b, s, d), BF16),
        compiler_params=_cparams(("parallel", "parallel", "parallel")),
        name="gqa_attention",
    )(q, k, vt)


def _out1_kernel(x_ref, a_ref, mod_ref, wo_ref, gain_ref, w1_ref, w2_ref, fin_ref, o_ref):
    o = jnp.dot(a_ref[0], wo_ref[...], preferred_element_type=F32)
    x1 = x_ref[0] + mod_ref[0, 0, 2:3, :] * o
    h2 = _norm_mod(x1, gain_ref[...], mod_ref[0, 0, 3:4, :], mod_ref[0, 0, 4:5, :])
    x2 = x1 + mod_ref[0, 0, 5:6, :] * _mlp(h2, w1_ref, w2_ref)
    ms = jnp.mean(x2 * x2, axis=-1, keepdims=True)
    o_ref[0] = x2 * lax.rsqrt(ms + NORM_EPS) * fin_ref[...]


def _out1(x2, att, mods, wo, gain_mlp, w1, w2, fin, ctx_tiles):
    b, s, d = att.shape
    weights = [wo, gain_mlp, w1, w2, fin]
    return pl.pallas_call(
        _out1_kernel,
        grid=(b, s // TM),
        in_specs=[pl.BlockSpec((1, TM, d), lambda bb, i: (bb, i + ctx_tiles, 0)),
                  pl.BlockSpec((1, TM, d), lambda bb, i: (bb, i, 0)),
                  pl.BlockSpec((1, 1, 6, d), lambda bb, i: (bb, 1, 0, 0))]
        + [_const_spec(w.shape) for w in weights],
        out_specs=pl.BlockSpec((1, TM, d), lambda bb, i: (bb, i, 0)),
        out_shape=jax.ShapeDtypeStruct((b, s, d), F32),
        compiler_params=_cparams(("parallel", "parallel")),
        name="attn_out_mlp_final",
    )(x2, att, mods, *weights)


def _block_diag2(a, b):
    za = jnp.zeros((a.shape[0], b.shape[1]), a.dtype)
    zb = jnp.zeros((b.shape[0], a.shape[1]), b.dtype)
    return jnp.concatenate([jnp.concatenate([a, za], 1), jnp.concatenate([zb, b], 1)], 0)


def _constants(d):
    lane = jnp.arange(d)[:, None] // HEAD
    seg = (lane == jnp.arange(SEG_W)[None, :]).astype(BF16)
    nk = KV_HEADS * HEAD
    segk = seg[:nk]
    t = jnp.arange(TM)
    same = (t[:, None] // CHUNK) == (t[None, :] // CHUNK)
    tri_f = (same & (t[None, :] <= t[:, None])).astype(BF16)
    tri_b = (same & (t[None, :] >= t[:, None])).astype(BF16)
    return dict(seg=seg, segt=seg.T, segk=segk, segkt=segk.T, tri_f=tri_f, tri_b=tri_b)


def _rope_tables(s, c_len):
    tok = jnp.arange(s)
    row = (tok // GRID_W).astype(F32)
    col = (tok % GRID_W).astype(F32)
    half = HEAD // 2
    freqs = ROPE_THETA ** (-jnp.arange(0, half, 2, dtype=F32) / half)
    ang_r = row[:, None] * freqs
    ang_c = col[:, None] * freqs
    cos = jnp.concatenate([jnp.cos(ang_r)] * 2 + [jnp.cos(ang_c)] * 2, axis=1)
    sin = jnp.concatenate([-jnp.sin(ang_r), jnp.sin(ang_r), -jnp.sin(ang_c), jnp.sin(ang_c)], axis=1)
    cos = jnp.concatenate([jnp.ones((c_len, HEAD), F32), cos], axis=0)
    sin = jnp.concatenate([jnp.zeros((c_len, HEAD), F32), sin], axis=0)
    return jnp.tile(cos, (1, 2)), jnp.tile(sin, (1, 2))


def kernel(x, c, ctx, c_ctx, w_mod, b_mod, norm_mix, norm_mlp, mlp_w1, mlp_w2, rwkv_mu, rwkv_wr, rwkv_wk, rwkv_wv,
           rwkv_wo, rwkv_w0, rwkv_w1, rwkv_w2, rwkv_a0, rwkv_a1, rwkv_a2, rwkv_g1, rwkv_g2, rwkv_k_k, rwkv_k_a,
           rwkv_r_k, rwkv_ln_w, rwkv_ln_b, attn_wqkv, attn_q_norm, attn_k_norm, attn_wo, final_norm):
    b, s, d = x.shape
    c_len = ctx.shape[1]
    assert d % (4 * 128) == 0 and d // HEAD <= SEG_W
    assert s % TQ == 0 and TQ % TM == 0 and TM % GRID_W == 0 and TM % CHUNK == 0
    assert c_len == TM, "the context must fill exactly one token tile"
    assert w_mod.shape[0] == 2 and rwkv_mu.shape[0] == 1 and attn_wqkv.shape[0] == 1
    ctx_tiles = c_len // TM
    row = lambda a: a.reshape(1, -1).astype(F32)
    bf = lambda a: a.astype(BF16)
    consts = _constants(d)

    rows = -(-(b + 1) // 8) * 8
    cvec = jnp.zeros((rows, d), F32).at[:b].set(c).at[b].set(c_ctx)
    m = _modulation(cvec, w_mod, b_mod)
    m = m.reshape(2, rows, 6, d)
    mods = [jnp.stack([jnp.broadcast_to(m[l, b][None], (b, 6, d)), m[l, :b]], axis=1) for l in range(2)]

    p0 = dict(
        mu=rwkv_mu[0], wr=bf(rwkv_wr[0]), wk=bf(rwkv_wk[0]), wv=bf(rwkv_wv[0]), wo=bf(rwkv_wo[0]),
        g1=bf(rwkv_g1[0]), g2=bf(rwkv_g2[0]),
        w1=bf(jnp.concatenate([rwkv_w1[0, 0], rwkv_w1[0, 1]], axis=1)),
        w2=bf(_block_diag2(rwkv_w2[0, 0], rwkv_w2[0, 1])),
        w0=rwkv_w0[0].reshape(1, -1),
        a1=bf(jnp.concatenate([rwkv_a1[0, 0], rwkv_a1[0, 1]], axis=1)),
        a2=bf(_block_diag2(rwkv_a2[0, 0], rwkv_a2[0, 1])),
        a0=rwkv_a0[0].reshape(1, -1),
        k_k=row(rwkv_k_k[0]), k_a=row(rwkv_k_a[0]), r_k=row(rwkv_r_k[0]),
        ln_w=row(rwkv_ln_w[0]), ln_b=row(rwkv_ln_b[0]),
    )
    (v, g, bonus, a_f, r_f, b_f, k_f, a_b, r_b, b_b, k_b, dl_f, dl_b) = _proj0(
        ctx, x, mods[0], row(norm_mix[0]), p0, consts)
    w_f, u_f = _wkv_intra(False, a_f, b_f, k_f, v)
    w_b, u_b = _wkv_intra(True, a_b, b_b, k_b, v)
    y_f, y_b = _wkv_scan(ctx_tiles, (w_f, u_f, r_f, b_f, k_f, v, dl_f), (w_b, u_b, r_b, b_b, k_b, v, dl_b))
    x1 = _readout0(ctx, x, y_f, y_b, bonus, g, mods[0], p0, row(norm_mlp[0]), bf(mlp_w1[0]), bf(mlp_w2[0]),
                   consts)

    cos, sin = _rope_tables(s, c_len)
    qn = jnp.tile(attn_q_norm[0], d // HEAD).reshape(1, -1)
    kn = jnp.tile(attn_k_norm[0], KV_HEADS).reshape(1, -1)
    nqk = d + KV_HEADS * HEAD
    wqkv = attn_wqkv[0]
    q, k, vt = _qkv1(x1, mods[1], row(norm_mix[1]), bf(wqkv[:, :d]), bf(wqkv[:, d:nqk]), bf(wqkv[:, nqk:].T),
                     qn, kn, cos, sin, consts, ctx_tiles)
    att = _attention(q, k, vt)
    return _out1(x1, att, mods[1], bf(attn_wo[0]), row(norm_mlp[1]), bf(mlp_w1[1]), bf(mlp_w2[1]),
                 row(final_norm), ctx_tiles)
```

```python
import functools
import math

import jax
import jax.numpy as jnp
from jax import lax
from jax.experimental import pallas as pl
from jax.experimental.pallas import tpu as pltpu

F32 = jnp.float32
BF16 = jnp.bfloat16

HEAD = 64
GRID_W = 64
KV_HEADS = 4
NORM_EPS = 1e-6
GN_EPS = 64e-5
ROPE_THETA = 10000.0
CHUNK = 64
TM = 256
SEG_W = 128
VMEM_LIMIT = 56 * 1024 * 1024
EXP_NEG_HALF = math.exp(-0.5)
LOG2E = math.log2(math.e)
KEY_BLOCK = 256
TQ = 256
VT_ROWS = 2 * HEAD
INTRA_CHUNKS_PER_ITER = 4
MLP_FF_CHUNK = 1024


def _bdot(a, b):
    return jnp.dot(a.astype(BF16), b.astype(BF16), preferred_element_type=F32)


def _split_hi_lo(x):
    hi = x.astype(BF16)
    lo = (x - hi.astype(F32)).astype(BF16)
    return hi, lo


def _seg_sum(x, seg, segt, split_input=True):
    if split_input:
        hi, lo = _split_hi_lo(x)
        s = jnp.dot(hi, seg, preferred_element_type=F32) + jnp.dot(lo, seg, preferred_element_type=F32)
    else:
        s = jnp.dot(x.astype(BF16), seg, preferred_element_type=F32)
    s_hi, s_lo = _split_hi_lo(s)
    return jnp.dot(s_hi, segt, preferred_element_type=F32) + jnp.dot(s_lo, segt, preferred_element_type=F32)


def _sigmoid(x):
    return 0.5 * jnp.tanh(0.5 * x) + 0.5


def _norm_mod(x, gain, shift, scale):
    ms = jnp.mean(x * x, axis=-1, keepdims=True)
    return x * lax.rsqrt(ms + NORM_EPS) * (gain * (1.0 + scale)) + shift


def _cparams(sem):
    return pltpu.CompilerParams(dimension_semantics=sem, vmem_limit_bytes=VMEM_LIMIT)


def _const_spec(shape):
    nd = len(shape)
    return pl.BlockSpec(shape, lambda *_: (0,) * nd)


def _mod_kernel(c_ref, w_ref, b_ref, o_ref):
    c = c_ref[...]
    s = c * jax.nn.sigmoid(c)
    s_hi, s_lo = _split_hi_lo(s)
    w = w_ref[0]
    w_hi, w_lo = _split_hi_lo(w)
    acc = jnp.dot(s_hi, w_hi, preferred_element_type=F32)
    acc += jnp.dot(s_hi, w_lo, preferred_element_type=F32)
    acc += jnp.dot(s_lo, w_hi, preferred_element_type=F32)
    o_ref[0] = acc + b_ref[0]


def _modulation(cvec, w_mod, b_mod):
    depth, d, n = w_mod.shape
    rows = cvec.shape[0]
    tn = 512
    return pl.pallas_call(
        _mod_kernel,
        grid=(depth, n // tn),
        in_specs=[
            pl.BlockSpec((rows, d), lambda l, j: (0, 0)),
            pl.BlockSpec((1, d, tn), lambda l, j: (l, 0, j)),
            pl.BlockSpec((1, 1, tn), lambda l, j: (l, 0, j)),
        ],
        out_specs=pl.BlockSpec((1, rows, tn), lambda l, j: (l, 0, j)),
        out_shape=jax.ShapeDtypeStruct((depth, rows, n), F32),
        compiler_params=_cparams(("parallel", "parallel")),
        name="adaln_modulation",
    )(cvec, w_mod, b_mod.reshape(depth, 1, n))


def _proj0_kernel(nt, cx_ref, xc_ref, xp_ref, xn_ref, mod_ref, gain_ref, mu_ref, wr_ref, wk_ref, wv_ref, g1_ref,
                  g2_ref, w1_ref, w2_ref, w0_ref, a1_ref, a2_ref, a0_ref, kk_ref, ka_ref, rk_ref, seg_ref, segt_ref,
                  trif_ref, trib_ref,
                  v_out, g_out, bonus_out, af_out, rf_out, bf_out, kf_out, ab_out, rb_out, bb_out, kb_out,
                  dlf_out, dlb_out, hs_ref):
    i = pl.program_id(1)
    d = xc_ref.shape[-1]
    q = d // 4
    gain = gain_ref[...]
    shift = mod_ref[0, 0, 0:1, :]
    scale = mod_ref[0, 0, 1:2, :]
    h = _norm_mod(jnp.where(i == 0, cx_ref[0], xc_ref[0]), gain, shift, scale)
    t_idx = lax.broadcasted_iota(jnp.int32, (TM, 1), 0)

    @pl.when(i == 0)
    def _():
        prev = jnp.where(t_idx == 0, 0.0, pltpu.roll(h[:, :2 * q], 1, 0))
        nxt = jnp.where(t_idx == TM - 1, 0.0, pltpu.roll(h[:, 2 * q:], TM - 1, 0))
        hs_ref[:, :2 * q] = prev
        hs_ref[:, 2 * q:] = nxt

    @pl.when(i > 0)
    def _():
        col = t_idx % GRID_W
        left = jnp.where(col == 0, 0.0, pltpu.roll(h[:, :q], 1, 0))
        right = jnp.where(col == GRID_W - 1, 0.0, pltpu.roll(h[:, q:2 * q], TM - 1, 0))
        hp = _norm_mod(xp_ref[0], gain, shift, scale)[:, 2 * q:3 * q]
        hn = _norm_mod(xn_ref[0], gain, shift, scale)[:, 3 * q:]
        hp = jnp.where(i > 1, hp, 0.0)
        hn = jnp.where(i < nt - 1, hn, 0.0)
        hs_ref[:, :q] = left
        hs_ref[:, q:2 * q] = right
        hs_ref[:, 2 * q:3 * q] = jnp.concatenate([hp, h[:TM - GRID_W, 2 * q:3 * q]], axis=0)
        hs_ref[:, 3 * q:] = jnp.concatenate([h[GRID_W:, 3 * q:], hn], axis=0)

    mub = mu_ref[...].astype(BF16)
    seg = seg_ref[...]
    segt = segt_ref[...]
    hb = h.astype(BF16)
    xxb = (hs_ref[...] - h).astype(BF16)
    mix = lambda j: hb + xxb * mub[j:j + 1, :]

    wl = _bdot(jnp.tanh(jnp.dot(mix(1), w1_ref[...], preferred_element_type=F32)), w2_ref[...]) + w0_ref[...]
    al = _bdot(jnp.dot(mix(4), a1_ref[...], preferred_element_type=F32), a2_ref[...]) + a0_ref[...]
    e = EXP_NEG_HALF * _sigmoid(wl)
    e_hi, e_lo = _split_hi_lo(e)
    dirs = []
    for dr, (tri_ref, dl_o) in enumerate(((trif_ref, dlf_out), (trib_ref, dlb_out))):
        tri = tri_ref[...]
        cols = slice(dr * d, (dr + 1) * d)
        c = (jnp.dot(tri, e_hi[:, cols], preferred_element_type=F32)
             + jnp.dot(tri, e_lo[:, cols], preferred_element_type=F32))
        grow = jnp.exp(c)
        shrink = jnp.exp(-c)
        last = CHUNK - 1 if dr == 0 else 0
        dl_o[0, 0] = jnp.concatenate(
            [shrink[n * CHUNK + last:n * CHUNK + last + 1, :] for n in range(TM // CHUNK)], axis=0)
        dirs.append((_sigmoid(al[:, cols]), grow, shrink, jnp.exp(e[:, cols] - c)))

    g = _bdot(_sigmoid(jnp.dot(mix(5), g1_ref[...], preferred_element_type=F32)), g2_ref[...])
    g_out[0] = g.astype(BF16)
    k = jnp.dot(mix(2), wk_ref[...], preferred_element_type=F32)
    r = jnp.dot(mix(0), wr_ref[...], preferred_element_type=F32)
    v = jnp.dot(mix(3), wv_ref[...], preferred_element_type=F32)
    v_out[0] = v.astype(BF16)

    kk = k * kk_ref[...]
    kk = kk * lax.rsqrt(_seg_sum(kk * kk, seg, segt, split_input=False) + 1e-12)
    ksum = None
    outs = ((af_out, rf_out, bf_out, kf_out), (ab_out, rb_out, bb_out, kb_out))
    for (a_o, r_o, b_o, k_o), (a_lr, grow, shrink, lag) in zip(outs, dirs):
        k_dir = k * (1.0 + (a_lr - 1.0) * ka_ref[...])
        ksum = k_dir if ksum is None else ksum + k_dir
        a_o[0] = (-kk * lag).astype(BF16)
        r_o[0] = (r * shrink).astype(BF16)
        b_o[0] = (kk * a_lr * grow).astype(BF16)
        k_o[0] = (k_dir * grow).astype(BF16)
    bonus_out[0] = (_seg_sum(r * ksum * rk_ref[...], seg, segt) * v).astype(BF16)


def _proj0(ctx, x, mods, gain, p, consts):
    b, s, d = x.shape
    t = s + ctx.shape[1]
    nt = t // TM
    hb = TM // GRID_W
    nhb = s // GRID_W
    tok = lambda bb, i: (bb, i, 0)
    big = jax.ShapeDtypeStruct((b, t, d), BF16)
    dl = jax.ShapeDtypeStruct((b, nt, TM // CHUNK, d), F32)
    tile_spec = pl.BlockSpec((1, TM, d), tok)
    dl_spec = pl.BlockSpec((1, 1, TM // CHUNK, d), lambda bb, i: (bb, i, 0, 0))
    weights = [p["mu"], p["wr"], p["wk"], p["wv"], p["g1"], p["g2"], p["w1"], p["w2"], p["w0"],
               p["a1"], p["a2"], p["a0"], p["k_k"], p["k_a"], p["r_k"],
               consts["seg"], consts["segt"], consts["tri_f"], consts["tri_b"]]
    return pl.pallas_call(
        functools.partial(_proj0_kernel, nt),
        grid=(b, nt),
        in_specs=[
            pl.BlockSpec((1, TM, d), lambda bb, i: (bb, 0, 0)),
            pl.BlockSpec((1, TM, d), lambda bb, i: (bb, jnp.maximum(i - 1, 0), 0)),
            pl.BlockSpec((1, GRID_W, d), lambda bb, i: (bb, jnp.maximum((i - 1) * hb - 1, 0), 0)),
            pl.BlockSpec((1, GRID_W, d), lambda bb, i: (bb, jnp.minimum(i * hb, nhb - 1), 0)),
            pl.BlockSpec((1, 1, 6, d), lambda bb, i: (bb, jnp.minimum(i, 1), 0, 0)),
            _const_spec(gain.shape),
        ] + [_const_spec(w.shape) for w in weights],
        out_specs=[tile_spec] * 11 + [dl_spec] * 2,
        out_shape=[big] * 11 + [dl] * 2,
        scratch_shapes=[pltpu.VMEM((TM, d), F32)],
        compiler_params=_cparams(("parallel", "parallel")),
        name="rwkv_project",
    )(ctx, x, x, x, mods, gain, *weights)


_NT = (((1,), (1,)), ((), ()))
_TN = (((0,), (0,)), ((), ()))
PAIR = 2 * HEAD


def _pair_masks(rows):
    lane = lax.broadcasted_iota(jnp.int32, (rows, PAIR), 1)
    return lane < HEAD, lane >= HEAD


def _tri_mask(reverse, inclusive):
    row = lax.broadcasted_iota(jnp.int32, (CHUNK, PAIR), 0)
    col = lax.broadcasted_iota(jnp.int32, (CHUNK, PAIR), 1) % HEAD
    if inclusive:
        return (col >= row) if reverse else (col <= row)
    return (col > row) if reverse else (col < row)


def _wkv_intra_kernel(reverse, a_ref, b_ref, k_ref, v_ref, w_out, u_out):
    npairs = a_ref.shape[-1] // PAIR
    h0, h1 = _pair_masks(CHUNK)
    lane2 = lax.broadcasted_iota(jnp.int32, (CHUNK, 2 * PAIR), 1)
    row2 = lax.broadcasted_iota(jnp.int32, (CHUNK, 2 * PAIR), 0)
    first = (lane2 // HEAD) % 2 == 0
    left = lane2 < PAIR
    col2 = lane2 % HEAD
    strict = (col2 > row2) if reverse else (col2 < row2)
    eye_right = ((col2 == row2) & (lane2 >= PAIR)).astype(F32)
    stack2 = lambda t: jnp.concatenate([jnp.where(h0, t, 0), jnp.where(h1, t, 0)], axis=0)

    per_iter = INTRA_CHUNKS_PER_ITER

    def chunk(ci, carry):
        rows = [pl.ds(pl.multiple_of((ci * per_iter + n) * CHUNK, CHUNK), CHUNK) for n in range(per_iter)]
        psl = lambda p: slice(p * PAIR, (p + 1) * PAIR)
        slabs = [(n, p) for n in range(per_iter) for p in range(npairs)]
        a = [a_ref[0, rows[n], psl(p)] for n, p in slabs]
        v = [v_ref[0, rows[n], psl(p)] for n, p in slabs]
        bk = [jnp.concatenate([stack2(b_ref[0, rows[n], psl(p)]), stack2(k_ref[0, rows[n], psl(p)])], axis=0)
              for n, p in slabs]
        sc = [jnp.where(strict, lax.dot_general(a[g], bk[g], _NT, preferred_element_type=F32), 0.0)
              for g in range(len(slabs))]
        x = [jnp.where(left, t, eye_right) for t in sc]
        for _ in range(6):
            xb = [t.astype(BF16) for t in x]
            wts = [jnp.concatenate([jnp.where(first, t, 0), jnp.where(first, 0, t)], axis=0) for t in xb]
            res = [jnp.dot(t[:, :PAIR], w, preferred_element_type=F32) for t, w in zip(xb, wts)]
            x = [r + jnp.where(left, 0.0, t) for r, t in zip(res, x)]
        tinv = [t[:, PAIR:].astype(BF16) for t in x]
        akv = [jnp.dot(sc[g][:, PAIR:].astype(BF16), stack2(v[g]), preferred_element_type=F32).astype(BF16)
               for g in range(len(slabs))]
        wu = [jnp.dot(tinv[g], jnp.concatenate([stack2(a[g]), stack2(akv[g])], axis=1),
                      preferred_element_type=F32) for g in range(len(slabs))]
        for n in range(per_iter):
            mine = wu[n * npairs:(n + 1) * npairs]
            w_out[0, rows[n], :] = jnp.concatenate([t[:, :PAIR] for t in mine], axis=1).astype(BF16)
            u_out[0, rows[n], :] = jnp.concatenate([t[:, PAIR:] for t in mine], axis=1).astype(BF16)
        return carry

    lax.fori_loop(0, a_ref.shape[1] // (CHUNK * per_iter), chunk, 0)


def _wkv_intra(reverse, a, bm, k, v):
    b, t, d = a.shape
    tile = pl.BlockSpec((1, TM, d), lambda bb, i: (bb, i, 0))
    return pl.pallas_call(
        functools.partial(_wkv_intra_kernel, reverse),
        grid=(b, t // TM),
        in_specs=[tile] * 4,
        out_specs=[tile, tile],
        out_shape=[jax.ShapeDtypeStruct((b, t, d), BF16)] * 2,
        compiler_params=_cparams(("parallel", "parallel")),
        name="wkv_intra_bwd" if reverse else "wkv_intra_fwd",
    )(a, bm, k, v)


def _wkv_scan_kernel(wf, uf, rf, bf, kf, vf, dlf, wb, ub, rb, bb, kb, vb, dlb, yf_ref, yb_ref, z_ref):
    dirs = ((False, wf, uf, rf, bf, kf, vf, dlf, yf_ref), (True, wb, ub, rb, bb, kb, vb, dlb, yb_ref))
    npairs = wf.shape[-1] // PAIR
    nchunks = wf.shape[1] // CHUNK
    slabs = [(dd, p) for dd in range(2) for p in range(npairs)]
    h0, h1 = _pair_masks(CHUNK)
    stack2 = lambda t: jnp.concatenate([jnp.where(h0, t, 0), jnp.where(h1, t, 0)], axis=0)
    lane2 = lax.broadcasted_iota(jnp.int32, (CHUNK, 2 * PAIR), 1) % HEAD
    row2 = lax.broadcasted_iota(jnp.int32, (CHUNK, 2 * PAIR), 0)
    incl = (lane2 <= row2, lane2 >= row2)
    brow = lax.broadcasted_iota(jnp.int32, (PAIR, PAIR), 0) // HEAD
    bcol = lax.broadcasted_iota(jnp.int32, (PAIR, PAIR), 1) // HEAD
    same_head = brow == bcol
    psl = lambda p: slice(p * PAIR, (p + 1) * PAIR)

    @pl.when(pl.program_id(1) == 0)
    def _():
        z_ref[...] = jnp.zeros(z_ref.shape, F32)

    z = [z_ref[dd, p] for dd, p in slabs]
    for n in range(nchunks):
        rows = [slice((nchunks - 1 - n) * CHUNK, (nchunks - n) * CHUNK) if dirs[dd][0]
                else slice(n * CHUNK, (n + 1) * CHUNK) for dd, _ in slabs]
        cidx = [(nchunks - 1 - n) if dirs[dd][0] else n for dd, _ in slabs]
        ld = lambda pos, g: dirs[slabs[g][0]][pos][0, rows[g], psl(slabs[g][1])]
        ng = len(slabs)
        r = [ld(3, g) for g in range(ng)]
        v = [ld(6, g) for g in range(ng)]
        bm = [ld(4, g) for g in range(ng)]
        km = [ld(5, g) for g in range(ng)]
        bk = [jnp.concatenate([bm[g], km[g]], axis=0) for g in range(ng)]
        bk4 = [jnp.concatenate([stack2(bm[g]), stack2(km[g])], axis=0) for g in range(ng)]
        wr = [jnp.concatenate([ld(1, g), r[g]], axis=0) for g in range(ng)]
        sr = [jnp.where(incl[slabs[g][0]], lax.dot_general(r[g], bk4[g], _NT, preferred_element_type=F32),
                        0.0).astype(BF16) for g in range(ng)]
        ws = [lax.dot_general(wr[g], z[g].astype(BF16), _NT, preferred_element_type=F32) for g in range(ng)]
        ub = [(ws[g][:CHUNK] + ld(2, g)).astype(BF16) for g in range(ng)]
        uv = [jnp.concatenate([ub[g], v[g]], axis=0) for g in range(ng)]
        uv4 = [jnp.concatenate([stack2(ub[g]), stack2(v[g])], axis=0) for g in range(ng)]
        ys = [ws[g][CHUNK:] + jnp.dot(sr[g], uv4[g], preferred_element_type=F32) for g in range(ng)]
        upd = [lax.dot_general(uv[g], bk[g], _TN, preferred_element_type=F32) for g in range(ng)]
        z = [(z[g] + jnp.where(same_head, upd[g], 0.0))
             * dirs[slabs[g][0]][7][0, 0, cidx[g]:cidx[g] + 1, psl(slabs[g][1])] for g in range(ng)]
        for dd in range(2):
            dirs[dd][8][0, rows[dd * npairs], :] = jnp.concatenate(
                ys[dd * npairs:(dd + 1) * npairs], axis=1).astype(BF16)
    for g, (dd, p) in enumerate(slabs):
        z_ref[dd, p] = z[g]


def _wkv_scan(ctx_tiles, fwd, bwd):
    b, t, d = fwd[0].shape
    nt = t // TM
    blk_b = lambda j: jnp.where(j < ctx_tiles, ctx_tiles - 1 - j, nt - 1 - j + ctx_tiles)
    specs = []
    for blk in (lambda j: j, blk_b):
        tile = pl.BlockSpec((1, TM, d), lambda bb, j, blk=blk: (bb, blk(j), 0))
        dl_spec = pl.BlockSpec((1, 1, TM // CHUNK, d), lambda bb, j, blk=blk: (bb, blk(j), 0, 0))
        specs.append((tile, dl_spec))
    return pl.pallas_call(
        _wkv_scan_kernel,
        grid=(b, nt),
        in_specs=[specs[0][0]] * 6 + [specs[0][1]] + [specs[1][0]] * 6 + [specs[1][1]],
        out_specs=[specs[0][0], specs[1][0]],
        out_shape=[jax.ShapeDtypeStruct((b, t, d), BF16)] * 2,
        scratch_shapes=[pltpu.VMEM((2, d // PAIR, PAIR, PAIR), F32)],
        compiler_params=_cparams(("parallel", "arbitrary")),
        name="wkv_scan",
    )(*fwd, *bwd)


def _mlp(h, w1_ref, w2_ref):
    hb = h.astype(BF16)
    dff = w1_ref.shape[1]
    fc = MLP_FF_CHUNK
    acc = None
    for j in range(dff // fc):
        a = jnp.dot(hb, w1_ref[:, j * fc:(j + 1) * fc], preferred_element_type=F32)
        a = jnp.square(jnp.maximum(a, 0.0)).astype(BF16)
        part = jnp.dot(a, w2_ref[j * fc:(j + 1) * fc, :], preferred_element_type=F32)
        acc = part if acc is None else acc + part
    return acc


def _readout0_kernel(ngroups, *refs):
    per_group = [refs[5 * n:5 * n + 5] for n in range(ngroups)]
    (mod_ref, lnw_ref, lnb_ref, wo_ref, gain_ref, w1_ref, w2_ref, seg_ref, segt_ref, o_ref) = refs[5 * ngroups:]
    seg = seg_ref[...]
    segt = segt_ref[...]
    x1s = []
    for x_ref, yf_ref, yb_ref, bonus_ref, g_ref in per_group:
        y = yf_ref[0].astype(F32) + yb_ref[0].astype(F32)
        mean = _seg_sum(y, seg, segt, split_input=False) * (1.0 / HEAD)
        dev = y - mean
        var = _seg_sum(dev * dev, seg, segt, split_input=False) * (1.0 / HEAD)
        yn = dev * lax.rsqrt(var + GN_EPS) * lnw_ref[...] + lnb_ref[...]
        mixed = (yn + bonus_ref[0].astype(F32)) * g_ref[0].astype(F32)
        x1s.append(x_ref[0] + mod_ref[0, 0, 2:3, :] * _bdot(mixed, wo_ref[...]))
    h2s = [_norm_mod(x1, gain_ref[...], mod_ref[0, 0, 3:4, :], mod_ref[0, 0, 4:5, :]) for x1 in x1s]
    for n, (x1, h2) in enumerate(zip(x1s, h2s)):
        o_ref[0, n * TM:(n + 1) * TM, :] = x1 + mod_ref[0, 0, 5:6, :] * _mlp(h2, w1_ref, w2_ref)


def _readout0(x, tile0, ngroups, mod_row, yf, yb, bonus, g, mods, p, gain_mlp, w1, w2, consts):
    b, n, d = x.shape
    rows = ngroups * TM
    weights = [p["ln_w"], p["ln_b"], p["wo"], gain_mlp, w1, w2, consts["seg"], consts["segt"]]
    in_specs, args = [], []
    for m in range(ngroups):
        in_specs.append(pl.BlockSpec((1, TM, d), lambda bb, i, m=m: (bb, ngroups * i + m, 0)))
        in_specs += [pl.BlockSpec((1, TM, d), lambda bb, i, m=m: (bb, ngroups * i + m + tile0, 0))] * 4
        args += [x, yf, yb, bonus, g]
    return pl.pallas_call(
        functools.partial(_readout0_kernel, ngroups),
        grid=(b, n // rows),
        in_specs=in_specs + [pl.BlockSpec((1, 1, 6, d), lambda bb, i: (bb, mod_row, 0, 0))]
        + [_const_spec(w.shape) for w in weights],
        out_specs=pl.BlockSpec((1, rows, d), lambda bb, i: (bb, i, 0)),
        out_shape=jax.ShapeDtypeStruct((b, n, d), F32),
        compiler_params=_cparams(("parallel", "parallel")),
        name="rwkv_readout_mlp_ctx" if ngroups == 1 else "rwkv_readout_mlp",
    )(*args, mods, *weights)


def _rope(x, cos, sin_signed):
    lane = lax.broadcasted_iota(jnp.int32, (1, 128), 1)
    low = (lane & 16) == 0
    cols = []
    for j in range(x.shape[1] // 128):
        xj = x[:, j * 128:(j + 1) * 128]
        partner = jnp.where(low, pltpu.roll(xj, 128 - 16, 1), pltpu.roll(xj, 16, 1))
        cols.append(xj * cos + partner * sin_signed)
    return jnp.concatenate(cols, axis=1)


def _qkv1_kernel(cx_ref, x_ref, mod_ref, gain_ref, wq_ref, wk_ref, wvt_ref, qn_ref, kn_ref, cos_ref, sin_ref,
                 seg_ref, segt_ref, segk_ref, segkt_ref, q_out, k_out, vt_out):
    x_in = jnp.where(pl.program_id(1) == 0, cx_ref[0], x_ref[0])
    hb = _norm_mod(x_in, gain_ref[...], mod_ref[0, 0, 0:1, :], mod_ref[0, 0, 1:2, :]).astype(BF16)
    cos = cos_ref[...]
    sin = sin_ref[...]

    q = jnp.dot(hb, wq_ref[...], preferred_element_type=F32)
    q = q * lax.rsqrt(_seg_sum(q * q, seg_ref[...], segt_ref[...], split_input=False) * (1.0 / HEAD)
                      + NORM_EPS) * qn_ref[...]
    q_out[0] = (_rope(q, cos, sin) * (LOG2E * HEAD ** -0.5)).astype(BF16)

    k = jnp.dot(hb, wk_ref[...], preferred_element_type=F32)
    vt = lax.dot_general(wvt_ref[...], hb, _NT, preferred_element_type=F32).astype(BF16)
    k = k * lax.rsqrt(_seg_sum(k * k, segk_ref[...], segkt_ref[...], split_input=False) * (1.0 / HEAD)
                      + NORM_EPS) * kn_ref[...]
    k = _rope(k, cos, sin).astype(BF16)
    ones = jnp.ones((VT_ROWS - HEAD, vt.shape[1]), BF16)
    for gi in range(KV_HEADS):
        k_out[0, gi] = k[:, gi * HEAD:(gi + 1) * HEAD]
        vt_out[0, gi] = jnp.concatenate([vt[gi * HEAD:(gi + 1) * HEAD, :], ones], axis=0)


def _qkv1(x_ctx, x_lat, mods, gain, wq, wk, wvt, qn, kn, cos, sin, consts, ctx_tiles):
    b, s, d = x_lat.shape
    t = s + x_ctx.shape[1]
    nt = t // TM
    tab = pl.BlockSpec((TM, 128), lambda bb, i: (i, 0))
    weights = [gain, wq, wk, wvt, qn, kn]
    segs = [consts["seg"], consts["segt"], consts["segk"], consts["segkt"]]
    return pl.pallas_call(
        _qkv1_kernel,
        grid=(b, nt),
        in_specs=[pl.BlockSpec((1, TM, d), lambda bb, i: (bb, 0, 0)),
                  pl.BlockSpec((1, TM, d), lambda bb, i: (bb, jnp.maximum(i - ctx_tiles, 0), 0)),
                  pl.BlockSpec((1, 1, 6, d), lambda bb, i: (bb, jnp.minimum(i, 1), 0, 0))]
        + [_const_spec(w.shape) for w in weights] + [tab, tab] + [_const_spec(w.shape) for w in segs],
        out_specs=[pl.BlockSpec((1, TM, d), lambda bb, i: (bb, jnp.maximum(i - ctx_tiles, 0), 0)),
                   pl.BlockSpec((1, KV_HEADS, TM, HEAD), lambda bb, i: (bb, 0, i, 0)),
                   pl.BlockSpec((1, KV_HEADS, VT_ROWS, TM), lambda bb, i: (bb, 0, 0, i))],
        out_shape=[jax.ShapeDtypeStruct((b, t - ctx_tiles * TM, d), BF16),
                   jax.ShapeDtypeStruct((b, KV_HEADS, t, HEAD), BF16),
                   jax.ShapeDtypeStruct((b, KV_HEADS, VT_ROWS, t), BF16)],
        compiler_params=_cparams(("parallel", "arbitrary")),
        name="attn_qkv",
    )(x_ctx, x_lat, mods, *weights, cos, sin, *segs)


def _attn_kernel(q_ref, k_ref, vt_ref, o_ref):
    nheads = q_ref.shape[-1] // HEAD
    nkeys = k_ref.shape[2]
    bounds = list(range(0, nkeys, KEY_BLOCK)) + [nkeys]
    nblk = len(bounds) - 1
    q = [q_ref[0, :, j * HEAD:(j + 1) * HEAD] for j in range(nheads)]

    def scores(i):
        kb = k_ref[0, 0, bounds[i]:bounds[i + 1], :]
        return [lax.dot_general(kb, qj, _NT, preferred_element_type=F32) for qj in q]

    m = [None] * nheads
    acc = [None] * nheads
    s_next = scores(0)
    for i in range(nblk):
        s = s_next
        if i + 1 < nblk:
            s_next = scores(i + 1)
        vb = vt_ref[0, 0, :, bounds[i]:bounds[i + 1]]
        for j in range(nheads):
            bmax = jnp.max(s[j], axis=0, keepdims=True)
            m_new = bmax if i == 0 else jnp.maximum(m[j], bmax)
            p = jnp.exp2(s[j] - m_new).astype(BF16)
            pv = jnp.dot(vb, p, preferred_element_type=F32)
            acc[j] = pv if i == 0 else acc[j] * jnp.exp2(m[j] - m_new) + pv
            m[j] = m_new
    outs = [a[:HEAD] / a[HEAD:HEAD + 1] for a in acc]
    o_ref[0] = jnp.concatenate(outs, axis=0).T.astype(BF16)


def _attention(q, k, vt):
    b, s, d = q.shape
    t = k.shape[2]
    gw = d // KV_HEADS
    return pl.pallas_call(
        _attn_kernel,
        grid=(b, KV_HEADS, s // TQ),
        in_specs=[pl.BlockSpec((1, TQ, gw), lambda bb, gi, i: (bb, i, gi)),
                  pl.BlockSpec((1, 1, t, HEAD), lambda bb, gi, i: (bb, gi, 0, 0)),
                  pl.BlockSpec((1, 1, VT_ROWS, t), lambda bb, gi, i: (bb, gi, 0, 0))],
        out_specs=pl.BlockSpec((1, TQ, gw), lambda bb, gi, i: (bb, i, gi)),
        out_shape=jax.ShapeDtypeStruct((b, s, d), BF16),
        compiler_params=_cparams(("parallel", "parallel", "parallel")),
        name="gqa_attention",
    )(q, k, vt)


def _out1_kernel(x_ref, a_ref, mod_ref, wo_ref, gain_ref, w1_ref, w2_ref, fin_ref, o_ref):
    groups = (slice(0, TM), slice(TM, 2 * TM))
    o = [jnp.dot(a_ref[0, rs, :], wo_ref[...], preferred_element_type=F32) for rs in groups]
    x1 = [x_ref[0, rs, :] + mod_ref[0, 0, 2:3, :] * oo for rs, oo in zip(groups, o)]
    h2 = [_norm_mod(t, gain_ref[...], mod_ref[0, 0, 3:4, :], mod_ref[0, 0, 4:5, :]) for t in x1]
    for rs, t, h in zip(groups, x1, h2):
        x2 = t + mod_ref[0, 0, 5:6, :] * _mlp(h, w1_ref, w2_ref)
        ms = jnp.mean(x2 * x2, axis=-1, keepdims=True)
        o_ref[0, rs, :] = x2 * lax.rsqrt(ms + NORM_EPS) * fin_ref[...]


def _out1(x_lat, att, mods, wo, gain_mlp, w1, w2, fin):
    b, s, d = att.shape
    weights = [wo, gain_mlp, w1, w2, fin]
    pair = pl.BlockSpec((1, 2 * TM, d), lambda bb, i: (bb, i, 0))
    return pl.pallas_call(
        _out1_kernel,
        grid=(b, s // (2 * TM)),
        in_specs=[pair, pair, pl.BlockSpec((1, 1, 6, d), lambda bb, i: (bb, 1, 0, 0))]
        + [_const_spec(w.shape) for w in weights],
        out_specs=pair,
        out_shape=jax.ShapeDtypeStruct((b, s, d), F32),
        compiler_params=_cparams(("parallel", "parallel")),
        name="attn_out_mlp_final",
    )(x_lat, att, mods, *weights)


def _block_diag2(a, b):
    za = jnp.zeros((a.shape[0], b.shape[1]), a.dtype)
    zb = jnp.zeros((b.shape[0], a.shape[1]), b.dtype)
    return jnp.concatenate([jnp.concatenate([a, za], 1), jnp.concatenate([zb, b], 1)], 0)


def _constants(d):
    lane = jnp.arange(d)[:, None] // HEAD
    seg = (lane == jnp.arange(SEG_W)[None, :]).astype(BF16)
    nk = KV_HEADS * HEAD
    segk = seg[:nk]
    t = jnp.arange(TM)
    same = (t[:, None] // CHUNK) == (t[None, :] // CHUNK)
    tri_f = (same & (t[None, :] <= t[:, None])).astype(BF16)
    tri_b = (same & (t[None, :] >= t[:, None])).astype(BF16)
    return dict(seg=seg, segt=seg.T, segk=segk, segkt=segk.T, tri_f=tri_f, tri_b=tri_b)


def _rope_tables(s, c_len):
    tok = jnp.arange(s)
    row = (tok // GRID_W).astype(F32)
    col = (tok % GRID_W).astype(F32)
    half = HEAD // 2
    freqs = ROPE_THETA ** (-jnp.arange(0, half, 2, dtype=F32) / half)
    ang_r = row[:, None] * freqs
    ang_c = col[:, None] * freqs
    cos = jnp.concatenate([jnp.cos(ang_r)] * 2 + [jnp.cos(ang_c)] * 2, axis=1)
    sin = jnp.concatenate([-jnp.sin(ang_r), jnp.sin(ang_r), -jnp.sin(ang_c), jnp.sin(ang_c)], axis=1)
    cos = jnp.concatenate([jnp.ones((c_len, HEAD), F32), cos], axis=0)
    sin = jnp.concatenate([jnp.zeros((c_len, HEAD), F32), sin], axis=0)
    return jnp.tile(cos, (1, 2)), jnp.tile(sin, (1, 2))


def kernel(x, c, ctx, c_ctx, w_mod, b_mod, norm_mix, norm_mlp, mlp_w1, mlp_w2, rwkv_mu, rwkv_wr, rwkv_wk, rwkv_wv,
           rwkv_wo, rwkv_w0, rwkv_w1, rwkv_w2, rwkv_a0, rwkv_a1, rwkv_a2, rwkv_g1, rwkv_g2, rwkv_k_k, rwkv_k_a,
           rwkv_r_k, rwkv_ln_w, rwkv_ln_b, attn_wqkv, attn_q_norm, attn_k_norm, attn_wo, final_norm):
    b, s, d = x.shape
    c_len = ctx.shape[1]
    assert d % (4 * 128) == 0 and d // HEAD <= SEG_W
    assert s % TQ == 0 and TQ % TM == 0 and s % (2 * TM) == 0 and TM % GRID_W == 0 and TM % CHUNK == 0
    assert c_len == TM, "the context must fill exactly one token tile"
    assert w_mod.shape[0] == 2 and rwkv_mu.shape[0] == 1 and attn_wqkv.shape[0] == 1
    ctx_tiles = c_len // TM
    row = lambda a: a.reshape(1, -1).astype(F32)
    bf = lambda a: a.astype(BF16)
    consts = _constants(d)

    rows = -(-(b + 1) // 8) * 8
    cvec = jnp.zeros((rows, d), F32).at[:b].set(c).at[b].set(c_ctx)
    m = _modulation(cvec, w_mod, b_mod)
    m = m.reshape(2, rows, 6, d)
    mods = [jnp.stack([jnp.broadcast_to(m[l, b][None], (b, 6, d)), m[l, :b]], axis=1) for l in range(2)]

    p0 = dict(
        mu=rwkv_mu[0], wr=bf(rwkv_wr[0]), wk=bf(rwkv_wk[0]), wv=bf(rwkv_wv[0]), wo=bf(rwkv_wo[0]),
        g1=bf(rwkv_g1[0]), g2=bf(rwkv_g2[0]),
        w1=bf(jnp.concatenate([rwkv_w1[0, 0], rwkv_w1[0, 1]], axis=1)),
        w2=bf(_block_diag2(rwkv_w2[0, 0], rwkv_w2[0, 1])),
        w0=rwkv_w0[0].reshape(1, -1),
        a1=bf(jnp.concatenate([rwkv_a1[0, 0], rwkv_a1[0, 1]], axis=1)),
        a2=bf(_block_diag2(rwkv_a2[0, 0], rwkv_a2[0, 1])),
        a0=rwkv_a0[0].reshape(1, -1),
        k_k=row(rwkv_k_k[0]), k_a=row(rwkv_k_a[0]), r_k=row(rwkv_r_k[0]),
        ln_w=row(rwkv_ln_w[0]), ln_b=row(rwkv_ln_b[0]),
    )
    (v, g, bonus, a_f, r_f, b_f, k_f, a_b, r_b, b_b, k_b, dl_f, dl_b) = _proj0(
        ctx, x, mods[0], row(norm_mix[0]), p0, consts)
    w_f, u_f = _wkv_intra(False, a_f, b_f, k_f, v)
    w_b, u_b = _wkv_intra(True, a_b, b_b, k_b, v)
    y_f, y_b = _wkv_scan(ctx_tiles, (w_f, u_f, r_f, b_f, k_f, v, dl_f), (w_b, u_b, r_b, b_b, k_b, v, dl_b))
    readout = functools.partial(_readout0, yf=y_f, yb=y_b, bonus=bonus, g=g, mods=mods[0], p=p0,
                                gain_mlp=row(norm_mlp[0]), w1=bf(mlp_w1[0]), w2=bf(mlp_w2[0]), consts=consts)
    x1_ctx = readout(ctx, 0, 1, 0)
    x1_lat = readout(x, ctx_tiles, 2, 1)

    cos, sin = _rope_tables(s, c_len)
    qn = jnp.tile(attn_q_norm[0], d // HEAD).reshape(1, -1)
    kn = jnp.tile(attn_k_norm[0], KV_HEADS).reshape(1, -1)
    nqk = d + KV_HEADS * HEAD
    wqkv = attn_wqkv[0]
    q, k, vt = _qkv1(x1_ctx, x1_lat, mods[1], row(norm_mix[1]), bf(wqkv[:, :d]), bf(wqkv[:, d:nqk]),
                     bf(wqkv[:, nqk:].T), qn, kn, cos, sin, consts, ctx_tiles)
    att = _attention(q, k, vt)
    return _out1(x1_lat, att, mods[1], bf(attn_wo[0]), row(norm_mlp[1]), bf(mlp_w1[1]), bf(mlp_w2[1]),
                 row(final_norm))
```

```python
import functools
import math

import jax
import jax.numpy as jnp
from jax import lax
from jax.experimental import pallas as pl
from jax.experimental.pallas import tpu as pltpu

F32 = jnp.float32
BF16 = jnp.bfloat16

HEAD = 64
GRID_W = 64
KV_HEADS = 4
NORM_EPS = 1e-6
GN_EPS = 64e-5
ROPE_THETA = 10000.0
CHUNK = 64
TM = 256
SEG_W = 128
VMEM_LIMIT = 56 * 1024 * 1024
EXP_NEG_HALF = math.exp(-0.5)
LOG2E = math.log2(math.e)
KEY_BLOCK = 256
TQ = 256
VT_ROWS = 2 * HEAD
INTRA_CHUNKS_PER_ITER = 4
MLP_FF_CHUNK = 1024


def _bdot(a, b):
    return jnp.dot(a.astype(BF16), b.astype(BF16), preferred_element_type=F32)


def _split_hi_lo(x):
    hi = x.astype(BF16)
    lo = (x - hi.astype(F32)).astype(BF16)
    return hi, lo


def _seg_sum(x, seg, segt, split_input=True):
    if split_input:
        hi, lo = _split_hi_lo(x)
        s = jnp.dot(hi, seg, preferred_element_type=F32) + jnp.dot(lo, seg, preferred_element_type=F32)
    else:
        s = jnp.dot(x.astype(BF16), seg, preferred_element_type=F32)
    s_hi, s_lo = _split_hi_lo(s)
    return jnp.dot(s_hi, segt, preferred_element_type=F32) + jnp.dot(s_lo, segt, preferred_element_type=F32)


def _sigmoid(x):
    return 0.5 * jnp.tanh(0.5 * x) + 0.5


def _norm_mod(x, gain, shift, scale):
    ms = jnp.mean(x * x, axis=-1, keepdims=True)
    return x * lax.rsqrt(ms + NORM_EPS) * (gain * (1.0 + scale)) + shift


def _cparams(sem):
    return pltpu.CompilerParams(dimension_semantics=sem, vmem_limit_bytes=VMEM_LIMIT)


def _const_spec(shape):
    nd = len(shape)
    return pl.BlockSpec(shape, lambda *_: (0,) * nd)


def _mod_kernel(c_ref, w_ref, b_ref, o_ref):
    c = c_ref[...]
    s = c * jax.nn.sigmoid(c)
    s_hi, s_lo = _split_hi_lo(s)
    w = w_ref[0]
    w_hi, w_lo = _split_hi_lo(w)
    acc = jnp.dot(s_hi, w_hi, preferred_element_type=F32)
    acc += jnp.dot(s_hi, w_lo, preferred_element_type=F32)
    acc += jnp.dot(s_lo, w_hi, preferred_element_type=F32)
    o_ref[0] = acc + b_ref[0]


def _modulation(cvec, w_mod, b_mod):
    depth, d, n = w_mod.shape
    rows = cvec.shape[0]
    tn = 512
    return pl.pallas_call(
        _mod_kernel,
        grid=(depth, n // tn),
        in_specs=[
            pl.BlockSpec((rows, d), lambda l, j: (0, 0)),
            pl.BlockSpec((1, d, tn), lambda l, j: (l, 0, j)),
            pl.BlockSpec((1, 1, tn), lambda l, j: (l, 0, j)),
        ],
        out_specs=pl.BlockSpec((1, rows, tn), lambda l, j: (l, 0, j)),
        out_shape=jax.ShapeDtypeStruct((depth, rows, n), F32),
        compiler_params=_cparams(("parallel", "parallel")),
        name="adaln_modulation",
    )(cvec, w_mod, b_mod.reshape(depth, 1, n))


def _proj0_kernel(nt, cx_ref, xc_ref, xp_ref, xn_ref, mod_ref, gain_ref, mu_ref, wr_ref, wk_ref, wv_ref, g1_ref,
                  g2_ref, w1_ref, w2_ref, w0_ref, a1_ref, a2_ref, a0_ref, kk_ref, ka_ref, rk_ref, seg_ref, segt_ref,
                  trif_ref, trib_ref,
                  v_out, g_out, bonus_out, af_out, rf_out, bf_out, kf_out, ab_out, rb_out, bb_out, kb_out,
                  dlf_out, dlb_out, hs_ref):
    i = pl.program_id(1)
    d = xc_ref.shape[-1]
    q = d // 4
    gain = gain_ref[...]
    shift = mod_ref[0, 0, 0:1, :]
    scale = mod_ref[0, 0, 1:2, :]
    h = _norm_mod(jnp.where(i == 0, cx_ref[0], xc_ref[0]), gain, shift, scale)
    t_idx = lax.broadcasted_iota(jnp.int32, (TM, 1), 0)

    @pl.when(i == 0)
    def _():
        prev = jnp.where(t_idx == 0, 0.0, pltpu.roll(h[:, :2 * q], 1, 0))
        nxt = jnp.where(t_idx == TM - 1, 0.0, pltpu.roll(h[:, 2 * q:], TM - 1, 0))
        hs_ref[:, :2 * q] = prev
        hs_ref[:, 2 * q:] = nxt

    @pl.when(i > 0)
    def _():
        col = t_idx % GRID_W
        left = jnp.where(col == 0, 0.0, pltpu.roll(h[:, :q], 1, 0))
        right = jnp.where(col == GRID_W - 1, 0.0, pltpu.roll(h[:, q:2 * q], TM - 1, 0))
        hp = _norm_mod(xp_ref[0], gain, shift, scale)[:, 2 * q:3 * q]
        hn = _norm_mod(xn_ref[0], gain, shift, scale)[:, 3 * q:]
        hp = jnp.where(i > 1, hp, 0.0)
        hn = jnp.where(i < nt - 1, hn, 0.0)
        hs_ref[:, :q] = left
        hs_ref[:, q:2 * q] = right
        hs_ref[:, 2 * q:3 * q] = jnp.concatenate([hp, h[:TM - GRID_W, 2 * q:3 * q]], axis=0)
        hs_ref[:, 3 * q:] = jnp.concatenate([h[GRID_W:, 3 * q:], hn], axis=0)

    mub = mu_ref[...].astype(BF16)
    seg = seg_ref[...]
    segt = segt_ref[...]
    hb = h.astype(BF16)
    xxb = (hs_ref[...] - h).astype(BF16)
    mix = lambda j: hb + xxb * mub[j:j + 1, :]

    wl = _bdot(jnp.tanh(jnp.dot(mix(1), w1_ref[...], preferred_element_type=F32)), w2_ref[...]) + w0_ref[...]
    al = _bdot(jnp.dot(mix(4), a1_ref[...], preferred_element_type=F32), a2_ref[...]) + a0_ref[...]
    e = EXP_NEG_HALF * _sigmoid(wl)
    e_hi, e_lo = _split_hi_lo(e)
    dirs = []
    for dr, (tri_ref, dl_o) in enumerate(((trif_ref, dlf_out), (trib_ref, dlb_out))):
        tri = tri_ref[...]
        cols = slice(dr * d, (dr + 1) * d)
        c = (jnp.dot(tri, e_hi[:, cols], preferred_element_type=F32)
             + jnp.dot(tri, e_lo[:, cols], preferred_element_type=F32))
        grow = jnp.exp(c)
        shrink = jnp.exp(-c)
        last = CHUNK - 1 if dr == 0 else 0
        dl_o[0, 0] = jnp.concatenate(
            [shrink[n * CHUNK + last:n * CHUNK + last + 1, :] for n in range(TM // CHUNK)], axis=0)
        dirs.append((_sigmoid(al[:, cols]), grow, shrink, jnp.exp(e[:, cols] - c)))

    g = _bdot(_sigmoid(jnp.dot(mix(5), g1_ref[...], preferred_element_type=F32)), g2_ref[...])
    g_out[0] = g.astype(BF16)
    k = jnp.dot(mix(2), wk_ref[...], preferred_element_type=F32)
    r = jnp.dot(mix(0), wr_ref[...], preferred_element_type=F32)
    v = jnp.dot(mix(3), wv_ref[...], preferred_element_type=F32)
    v_out[0] = v.astype(BF16)

    kk = k * kk_ref[...]
    kk = kk * lax.rsqrt(_seg_sum(kk * kk, seg, segt, split_input=False) + 1e-12)
    ksum = None
    outs = ((af_out, rf_out, bf_out, kf_out), (ab_out, rb_out, bb_out, kb_out))
    for (a_o, r_o, b_o, k_o), (a_lr, grow, shrink, lag) in zip(outs, dirs):
        k_dir = k * (1.0 + (a_lr - 1.0) * ka_ref[...])
        ksum = k_dir if ksum is None else ksum + k_dir
        a_o[0] = (-kk * lag).astype(BF16)
        r_o[0] = (r * shrink).astype(BF16)
        b_o[0] = (kk * a_lr * grow).astype(BF16)
        k_o[0] = (k_dir * grow).astype(BF16)
    bonus_out[0] = (_seg_sum(r * ksum * rk_ref[...], seg, segt) * v).astype(BF16)


def _proj0(ctx, x, mods, gain, p, consts):
    b, s, d = x.shape
    t = s + ctx.shape[1]
    nt = t // TM
    hb = TM // GRID_W
    nhb = s // GRID_W
    tok = lambda bb, i: (bb, i, 0)
    big = jax.ShapeDtypeStruct((b, t, d), BF16)
    dl = jax.ShapeDtypeStruct((b, nt, TM // CHUNK, d), F32)
    tile_spec = pl.BlockSpec((1, TM, d), tok)
    dl_spec = pl.BlockSpec((1, 1, TM // CHUNK, d), lambda bb, i: (bb, i, 0, 0))
    weights = [p["mu"], p["wr"], p["wk"], p["wv"], p["g1"], p["g2"], p["w1"], p["w2"], p["w0"],
               p["a1"], p["a2"], p["a0"], p["k_k"], p["k_a"], p["r_k"],
               consts["seg"], consts["segt"], consts["tri_f"], consts["tri_b"]]
    return pl.pallas_call(
        functools.partial(_proj0_kernel, nt),
        grid=(b, nt),
        in_specs=[
            pl.BlockSpec((1, TM, d), lambda bb, i: (bb, 0, 0)),
            pl.BlockSpec((1, TM, d), lambda bb, i: (bb, jnp.maximum(i - 1, 0), 0)),
            pl.BlockSpec((1, GRID_W, d), lambda bb, i: (bb, jnp.maximum((i - 1) * hb - 1, 0), 0)),
            pl.BlockSpec((1, GRID_W, d), lambda bb, i: (bb, jnp.minimum(i * hb, nhb - 1), 0)),
            pl.BlockSpec((1, 1, 6, d), lambda bb, i: (bb, jnp.minimum(i, 1), 0, 0)),
            _const_spec(gain.shape),
        ] + [_const_spec(w.shape) for w in weights],
        out_specs=[tile_spec] * 11 + [dl_spec] * 2,
        out_shape=[big] * 11 + [dl] * 2,
        scratch_shapes=[pltpu.VMEM((TM, d), F32)],
        compiler_params=_cparams(("parallel", "parallel")),
        name="rwkv_project",
    )(ctx, x, x, x, mods, gain, *weights)


_NT = (((1,), (1,)), ((), ()))
_TN = (((0,), (0,)), ((), ()))
PAIR = 2 * HEAD


def _pair_masks(rows):
    lane = lax.broadcasted_iota(jnp.int32, (rows, PAIR), 1)
    return lane < HEAD, lane >= HEAD


def _tri_mask(reverse, inclusive):
    row = lax.broadcasted_iota(jnp.int32, (CHUNK, PAIR), 0)
    col = lax.broadcasted_iota(jnp.int32, (CHUNK, PAIR), 1) % HEAD
    if inclusive:
        return (col >= row) if reverse else (col <= row)
    return (col > row) if reverse else (col < row)


def _wkv_intra_kernel(reverse, a_ref, b_ref, k_ref, v_ref, w_out, u_out):
    npairs = a_ref.shape[-1] // PAIR
    h0, h1 = _pair_masks(CHUNK)
    lane2 = lax.broadcasted_iota(jnp.int32, (CHUNK, 2 * PAIR), 1)
    row2 = lax.broadcasted_iota(jnp.int32, (CHUNK, 2 * PAIR), 0)
    col2 = lane2 % HEAD
    strict = (col2 > row2) if reverse else (col2 < row2)
    row1 = lax.broadcasted_iota(jnp.int32, (CHUNK, PAIR), 0)
    col1 = lax.broadcasted_iota(jnp.int32, (CHUNK, PAIR), 1) % HEAD
    eye_pair = (row1 == col1).astype(F32)
    sizes = [2 ** n for n in range(CHUNK.bit_length() - 1)]
    off_masks = [((row1 // (2 * s)) == (col1 // (2 * s))) & ((row1 // s) != (col1 // s)) for s in sizes]
    stack2 = lambda t: jnp.concatenate([jnp.where(h0, t, 0), jnp.where(h1, t, 0)], axis=0)

    per_iter = INTRA_CHUNKS_PER_ITER

    def chunk(ci, carry):
        rows = [pl.ds(pl.multiple_of((ci * per_iter + n) * CHUNK, CHUNK), CHUNK) for n in range(per_iter)]
        psl = lambda p: slice(p * PAIR, (p + 1) * PAIR)
        slabs = [(n, p) for n in range(per_iter) for p in range(npairs)]
        a = [a_ref[0, rows[n], psl(p)] for n, p in slabs]
        v = [v_ref[0, rows[n], psl(p)] for n, p in slabs]
        bk = [jnp.concatenate([stack2(b_ref[0, rows[n], psl(p)]), stack2(k_ref[0, rows[n], psl(p)])], axis=0)
              for n, p in slabs]
        sc = [jnp.where(strict, lax.dot_general(a[g], bk[g], _NT, preferred_element_type=F32), 0.0)
              for g in range(len(slabs))]
        amat = [t[:, :PAIR] for t in sc]
        tin = [eye_pair + jnp.where(off_masks[0], t, 0.0) for t in amat]
        for off in off_masks[1:]:
            tb = [t.astype(BF16) for t in tin]
            y = [jnp.dot(jnp.where(off, t, 0.0).astype(BF16), stack2(u), preferred_element_type=F32)
                 for t, u in zip(amat, tb)]
            z = [jnp.dot(u, stack2(t.astype(BF16)), preferred_element_type=F32) for u, t in zip(tb, y)]
            tin = [t + dz for t, dz in zip(tin, z)]
        tinv = [t.astype(BF16) for t in tin]
        akv = [jnp.dot(sc[g][:, PAIR:].astype(BF16), stack2(v[g]), preferred_element_type=F32).astype(BF16)
               for g in range(len(slabs))]
        wu = [jnp.dot(tinv[g], jnp.concatenate([stack2(a[g]), stack2(akv[g])], axis=1),
                      preferred_element_type=F32) for g in range(len(slabs))]
        for n in range(per_iter):
            mine = wu[n * npairs:(n + 1) * npairs]
            w_out[0, rows[n], :] = jnp.concatenate([t[:, :PAIR] for t in mine], axis=1).astype(BF16)
            u_out[0, rows[n], :] = jnp.concatenate([t[:, PAIR:] for t in mine], axis=1).astype(BF16)
        return carry

    lax.fori_loop(0, a_ref.shape[1] // (CHUNK * per_iter), chunk, 0)


def _wkv_intra(reverse, a, bm, k, v):
    b, t, d = a.shape
    tile = pl.BlockSpec((1, TM, d), lambda bb, i: (bb, i, 0))
    return pl.pallas_call(
        functools.partial(_wkv_intra_kernel, reverse),
        grid=(b, t // TM),
        in_specs=[tile] * 4,
        out_specs=[tile, tile],
        out_shape=[jax.ShapeDtypeStruct((b, t, d), BF16)] * 2,
        compiler_params=_cparams(("parallel", "parallel")),
        name="wkv_intra_bwd" if reverse else "wkv_intra_fwd",
    )(a, bm, k, v)


def _wkv_scan_kernel(wf, uf, rf, bf, kf, vf, dlf, wb, ub, rb, bb, kb, vb, dlb, yf_ref, yb_ref, z_ref):
    dirs = ((False, wf, uf, rf, bf, kf, vf, dlf, yf_ref), (True, wb, ub, rb, bb, kb, vb, dlb, yb_ref))
    npairs = wf.shape[-1] // PAIR
    nchunks = wf.shape[1] // CHUNK
    slabs = [(dd, p) for dd in range(2) for p in range(npairs)]
    h0, h1 = _pair_masks(CHUNK)
    stack2 = lambda t: jnp.concatenate([jnp.where(h0, t, 0), jnp.where(h1, t, 0)], axis=0)
    lane2 = lax.broadcasted_iota(jnp.int32, (CHUNK, 2 * PAIR), 1) % HEAD
    row2 = lax.broadcasted_iota(jnp.int32, (CHUNK, 2 * PAIR), 0)
    incl = (lane2 <= row2, lane2 >= row2)
    brow = lax.broadcasted_iota(jnp.int32, (PAIR, PAIR), 0) // HEAD
    bcol = lax.broadcasted_iota(jnp.int32, (PAIR, PAIR), 1) // HEAD
    same_head = brow == bcol
    psl = lambda p: slice(p * PAIR, (p + 1) * PAIR)

    @pl.when(pl.program_id(1) == 0)
    def _():
        z_ref[...] = jnp.zeros(z_ref.shape, F32)

    z = [z_ref[dd, p] for dd, p in slabs]
    for n in range(nchunks):
        rows = [slice((nchunks - 1 - n) * CHUNK, (nchunks - n) * CHUNK) if dirs[dd][0]
                else slice(n * CHUNK, (n + 1) * CHUNK) for dd, _ in slabs]
        cidx = [(nchunks - 1 - n) if dirs[dd][0] else n for dd, _ in slabs]
        ld = lambda pos, g: dirs[slabs[g][0]][pos][0, rows[g], psl(slabs[g][1])]
        ng = len(slabs)
        r = [ld(3, g) for g in range(ng)]
        v = [ld(6, g) for g in range(ng)]
        bm = [ld(4, g) for g in range(ng)]
        km = [ld(5, g) for g in range(ng)]
        bk = [jnp.concatenate([bm[g], km[g]], axis=0) for g in range(ng)]
        bk4 = [jnp.concatenate([stack2(bm[g]), stack2(km[g])], axis=0) for g in range(ng)]
        wr = [jnp.concatenate([ld(1, g), r[g]], axis=0) for g in range(ng)]
        sr = [jnp.where(incl[slabs[g][0]], lax.dot_general(r[g], bk4[g], _NT, preferred_element_type=F32),
                        0.0).astype(BF16) for g in range(ng)]
        ws = [lax.dot_general(wr[g], z[g].astype(BF16), _NT, preferred_element_type=F32) for g in range(ng)]
        ub = [(ws[g][:CHUNK] + ld(2, g)).astype(BF16) for g in range(ng)]
        uv = [jnp.concatenate([ub[g], v[g]], axis=0) for g in range(ng)]
        uv4 = [jnp.concatenate([stack2(ub[g]), stack2(v[g])], axis=0) for g in range(ng)]
        ys = [ws[g][CHUNK:] + jnp.dot(sr[g], uv4[g], preferred_element_type=F32) for g in range(ng)]
        upd = [lax.dot_general(uv[g], bk[g], _TN, preferred_element_type=F32) for g in range(ng)]
        z = [(z[g] + jnp.where(same_head, upd[g], 0.0))
             * dirs[slabs[g][0]][7][0, 0, cidx[g]:cidx[g] + 1, psl(slabs[g][1])] for g in range(ng)]
        for dd in range(2):
            dirs[dd][8][0, rows[dd * npairs], :] = jnp.concatenate(
                ys[dd * npairs:(dd + 1) * npairs], axis=1).astype(BF16)
    for g, (dd, p) in enumerate(slabs):
        z_ref[dd, p] = z[g]


def _wkv_scan(ctx_tiles, fwd, bwd):
    b, t, d = fwd[0].shape
    nt = t // TM
    blk_b = lambda j: jnp.where(j < ctx_tiles, ctx_tiles - 1 - j, nt - 1 - j + ctx_tiles)
    specs = []
    for blk in (lambda j: j, blk_b):
        tile = pl.BlockSpec((1, TM, d), lambda bb, j, blk=blk: (bb, blk(j), 0))
        dl_spec = pl.BlockSpec((1, 1, TM // CHUNK, d), lambda bb, j, blk=blk: (bb, blk(j), 0, 0))
        specs.append((tile, dl_spec))
    return pl.pallas_call(
        _wkv_scan_kernel,
        grid=(b, nt),
        in_specs=[specs[0][0]] * 6 + [specs[0][1]] + [specs[1][0]] * 6 + [specs[1][1]],
        out_specs=[specs[0][0], specs[1][0]],
        out_shape=[jax.ShapeDtypeStruct((b, t, d), BF16)] * 2,
        scratch_shapes=[pltpu.VMEM((2, d // PAIR, PAIR, PAIR), F32)],
        compiler_params=_cparams(("parallel", "arbitrary")),
        name="wkv_scan",
    )(*fwd, *bwd)


def _mlp(h, w1_ref, w2_ref):
    hb = h.astype(BF16)
    dff = w1_ref.shape[1]
    fc = MLP_FF_CHUNK
    acc = None
    for j in range(dff // fc):
        a = jnp.dot(hb, w1_ref[:, j * fc:(j + 1) * fc], preferred_element_type=F32)
        a = jnp.square(jnp.maximum(a, 0.0)).astype(BF16)
        part = jnp.dot(a, w2_ref[j * fc:(j + 1) * fc, :], preferred_element_type=F32)
        acc = part if acc is None else acc + part
    return acc


def _readout0_kernel(ngroups, *refs):
    per_group = [refs[5 * n:5 * n + 5] for n in range(ngroups)]
    (mod_ref, lnw_ref, lnb_ref, wo_ref, gain_ref, w1_ref, w2_ref, seg_ref, segt_ref, o_ref) = refs[5 * ngroups:]
    seg = seg_ref[...]
    segt = segt_ref[...]
    x1s = []
    for x_ref, yf_ref, yb_ref, bonus_ref, g_ref in per_group:
        y = yf_ref[0].astype(F32) + yb_ref[0].astype(F32)
        mean = _seg_sum(y, seg, segt, split_input=False) * (1.0 / HEAD)
        dev = y - mean
        var = _seg_sum(dev * dev, seg, segt, split_input=False) * (1.0 / HEAD)
        yn = dev * lax.rsqrt(var + GN_EPS) * lnw_ref[...] + lnb_ref[...]
        mixed = (yn + bonus_ref[0].astype(F32)) * g_ref[0].astype(F32)
        x1s.append(x_ref[0] + mod_ref[0, 0, 2:3, :] * _bdot(mixed, wo_ref[...]))
    h2s = [_norm_mod(x1, gain_ref[...], mod_ref[0, 0, 3:4, :], mod_ref[0, 0, 4:5, :]) for x1 in x1s]
    for n, (x1, h2) in enumerate(zip(x1s, h2s)):
        o_ref[0, n * TM:(n + 1) * TM, :] = x1 + mod_ref[0, 0, 5:6, :] * _mlp(h2, w1_ref, w2_ref)


def _readout0(x, tile0, ngroups, mod_row, yf, yb, bonus, g, mods, p, gain_mlp, w1, w2, consts):
    b, n, d = x.shape
    rows = ngroups * TM
    weights = [p["ln_w"], p["ln_b"], p["wo"], gain_mlp, w1, w2, consts["seg"], consts["segt"]]
    in_specs, args = [], []
    for m in range(ngroups):
        in_specs.append(pl.BlockSpec((1, TM, d), lambda bb, i, m=m: (bb, ngroups * i + m, 0)))
        in_specs += [pl.BlockSpec((1, TM, d), lambda bb, i, m=m: (bb, ngroups * i + m + tile0, 0))] * 4
        args += [x, yf, yb, bonus, g]
    return pl.pallas_call(
        functools.partial(_readout0_kernel, ngroups),
        grid=(b, n // rows),
        in_specs=in_specs + [pl.BlockSpec((1, 1, 6, d), lambda bb, i: (bb, mod_row, 0, 0))]
        + [_const_spec(w.shape) for w in weights],
        out_specs=pl.BlockSpec((1, rows, d), lambda bb, i: (bb, i, 0)),
        out_shape=jax.ShapeDtypeStruct((b, n, d), F32),
        compiler_params=_cparams(("parallel", "parallel")),
        name="rwkv_readout_mlp_ctx" if ngroups == 1 else "rwkv_readout_mlp",
    )(*args, mods, *weights)


def _rope(x, cos, sin_signed):
    lane = lax.broadcasted_iota(jnp.int32, (1, 128), 1)
    low = (lane & 16) == 0
    cols = []
    for j in range(x.shape[1] // 128):
        xj = x[:, j * 128:(j + 1) * 128]
        partner = jnp.where(low, pltpu.roll(xj, 128 - 16, 1), pltpu.roll(xj, 16, 1))
        cols.append(xj * cos + partner * sin_signed)
    return jnp.concatenate(cols, axis=1)


def _qkv1_kernel(cx_ref, x_ref, mod_ref, gain_ref, wq_ref, wk_ref, wvt_ref, qn_ref, kn_ref, cos_ref, sin_ref,
                 seg_ref, segt_ref, segk_ref, segkt_ref, q_out, k_out, vt_out):
    x_in = jnp.where(pl.program_id(1) == 0, cx_ref[0], x_ref[0])
    hb = _norm_mod(x_in, gain_ref[...], mod_ref[0, 0, 0:1, :], mod_ref[0, 0, 1:2, :]).astype(BF16)
    cos = cos_ref[...]
    sin = sin_ref[...]

    q = jnp.dot(hb, wq_ref[...], preferred_element_type=F32)
    q = q * lax.rsqrt(_seg_sum(q * q, seg_ref[...], segt_ref[...], split_input=False) * (1.0 / HEAD)
                      + NORM_EPS) * qn_ref[...]
    q_out[0] = (_rope(q, cos, sin) * (LOG2E * HEAD ** -0.5)).astype(BF16)

    k = jnp.dot(hb, wk_ref[...], preferred_element_type=F32)
    vt = lax.dot_general(wvt_ref[...], hb, _NT, preferred_element_type=F32).astype(BF16)
    k = k * lax.rsqrt(_seg_sum(k * k, segk_ref[...], segkt_ref[...], split_input=False) * (1.0 / HEAD)
                      + NORM_EPS) * kn_ref[...]
    k = _rope(k, cos, sin).astype(BF16)
    ones = jnp.ones((VT_ROWS - HEAD, vt.shape[1]), BF16)
    for gi in range(KV_HEADS):
        k_out[0, gi] = k[:, gi * HEAD:(gi + 1) * HEAD]
        vt_out[0, gi] = jnp.concatenate([vt[gi * HEAD:(gi + 1) * HEAD, :], ones], axis=0)


def _qkv1(x_ctx, x_lat, mods, gain, wq, wk, wvt, qn, kn, cos, sin, consts, ctx_tiles):
    b, s, d = x_lat.shape
    t = s + x_ctx.shape[1]
    nt = t // TM
    tab = pl.BlockSpec((TM, 128), lambda bb, i: (i, 0))
    weights = [gain, wq, wk, wvt, qn, kn]
    segs = [consts["seg"], consts["segt"], consts["segk"], consts["segkt"]]
    return pl.pallas_call(
        _qkv1_kernel,
        grid=(b, nt),
        in_specs=[pl.BlockSpec((1, TM, d), lambda bb, i: (bb, 0, 0)),
                  pl.BlockSpec((1, TM, d), lambda bb, i: (bb, jnp.maximum(i - ctx_tiles, 0), 0)),
                  pl.BlockSpec((1, 1, 6, d), lambda bb, i: (bb, jnp.minimum(i, 1), 0, 0))]
        + [_const_spec(w.shape) for w in weights] + [tab, tab] + [_const_spec(w.shape) for w in segs],
        out_specs=[pl.BlockSpec((1, TM, d), lambda bb, i: (bb, jnp.maximum(i - ctx_tiles, 0), 0)),
                   pl.BlockSpec((1, KV_HEADS, TM, HEAD), lambda bb, i: (bb, 0, i, 0)),
                   pl.BlockSpec((1, KV_HEADS, VT_ROWS, TM), lambda bb, i: (bb, 0, 0, i))],
        out_shape=[jax.ShapeDtypeStruct((b, t - ctx_tiles * TM, d), BF16),
                   jax.ShapeDtypeStruct((b, KV_HEADS, t, HEAD), BF16),
                   jax.ShapeDtypeStruct((b, KV_HEADS, VT_ROWS, t), BF16)],
        compiler_params=_cparams(("parallel", "arbitrary")),
        name="attn_qkv",
    )(x_ctx, x_lat, mods, *weights, cos, sin, *segs)


def _attn_kernel(q_ref, k_ref, vt_ref, o_ref):
    nheads = q_ref.shape[-1] // HEAD
    nkeys = k_ref.shape[2]
    bounds = list(range(0, nkeys, KEY_BLOCK)) + [nkeys]
    nblk = len(bounds) - 1
    q = [q_ref[0, :, j * HEAD:(j + 1) * HEAD] for j in range(nheads)]

    def scores(i):
        kb = k_ref[0, 0, bounds[i]:bounds[i + 1], :]
        return [lax.dot_general(kb, qj, _NT, preferred_element_type=F32) for qj in q]

    m = [None] * nheads
    acc = [None] * nheads
    s_next = scores(0)
    for i in range(nblk):
        s = s_next
        if i + 1 < nblk:
            s_next = scores(i + 1)
        vb = vt_ref[0, 0, :, bounds[i]:bounds[i + 1]]
        for j in range(nheads):
            bmax = jnp.max(s[j], axis=0, keepdims=True)
            m_new = bmax if i == 0 else jnp.maximum(m[j], bmax)
            p = jnp.exp2(s[j] - m_new).astype(BF16)
            pv = jnp.dot(vb, p, preferred_element_type=F32)
            acc[j] = pv if i == 0 else acc[j] * jnp.exp2(m[j] - m_new) + pv
            m[j] = m_new
    outs = [a[:HEAD] / a[HEAD:HEAD + 1] for a in acc]
    o_ref[0] = jnp.concatenate(outs, axis=0).T.astype(BF16)


def _attention(q, k, vt):
    b, s, d = q.shape
    t = k.shape[2]
    gw = d // KV_HEADS
    return pl.pallas_call(
        _attn_kernel,
        grid=(b, KV_HEADS, s // TQ),
        in_specs=[pl.BlockSpec((1, TQ, gw), lambda bb, gi, i: (bb, i, gi)),
                  pl.BlockSpec((1, 1, t, HEAD), lambda bb, gi, i: (bb, gi, 0, 0)),
                  pl.BlockSpec((1, 1, VT_ROWS, t), lambda bb, gi, i: (bb, gi, 0, 0))],
        out_specs=pl.BlockSpec((1, TQ, gw), lambda bb, gi, i: (bb, i, gi)),
        out_shape=jax.ShapeDtypeStruct((b, s, d), BF16),
        compiler_params=_cparams(("parallel", "parallel", "parallel")),
        name="gqa_attention",
    )(q, k, vt)


def _out1_kernel(x_ref, a_ref, mod_ref, wo_ref, gain_ref, w1_ref, w2_ref, fin_ref, o_ref):
    groups = (slice(0, TM), slice(TM, 2 * TM))
    o = [jnp.dot(a_ref[0, rs, :], wo_ref[...], preferred_element_type=F32) for rs in groups]
    x1 = [x_ref[0, rs, :] + mod_ref[0, 0, 2:3, :] * oo for rs, oo in zip(groups, o)]
    h2 = [_norm_mod(t, gain_ref[...], mod_ref[0, 0, 3:4, :], mod_ref[0, 0, 4:5, :]) for t in x1]
    for rs, t, h in zip(groups, x1, h2):
        x2 = t + mod_ref[0, 0, 5:6, :] * _mlp(h, w1_ref, w2_ref)
        ms = jnp.mean(x2 * x2, axis=-1, keepdims=True)
        o_ref[0, rs, :] = x2 * lax.rsqrt(ms + NORM_EPS) * fin_ref[...]


def _out1(x_lat, att, mods, wo, gain_mlp, w1, w2, fin):
    b, s, d = att.shape
    weights = [wo, gain_mlp, w1, w2, fin]
    pair = pl.BlockSpec((1, 2 * TM, d), lambda bb, i: (bb, i, 0))
    return pl.pallas_call(
        _out1_kernel,
        grid=(b, s // (2 * TM)),
        in_specs=[pair, pair, pl.BlockSpec((1, 1, 6, d), lambda bb, i: (bb, 1, 0, 0))]
        + [_const_spec(w.shape) for w in weights],
        out_specs=pair,
        out_shape=jax.ShapeDtypeStruct((b, s, d), F32),
        compiler_params=_cparams(("parallel", "parallel")),
        name="attn_out_mlp_final",
    )(x_lat, att, mods, *weights)


def _block_diag2(a, b):
    za = jnp.zeros((a.shape[0], b.shape[1]), a.dtype)
    zb = jnp.zeros((b.shape[0], a.shape[1]), b.dtype)
    return jnp.concatenate([jnp.concatenate([a, za], 1), jnp.concatenate([zb, b], 1)], 0)


def _constants(d):
    lane = jnp.arange(d)[:, None] // HEAD
    seg = (lane == jnp.arange(SEG_W)[None, :]).astype(BF16)
    nk = KV_HEADS * HEAD
    segk = seg[:nk]
    t = jnp.arange(TM)
    same = (t[:, None] // CHUNK) == (t[None, :] // CHUNK)
    tri_f = (same & (t[None, :] <= t[:, None])).astype(BF16)
    tri_b = (same & (t[None, :] >= t[:, None])).astype(BF16)
    return dict(seg=seg, segt=seg.T, segk=segk, segkt=segk.T, tri_f=tri_f, tri_b=tri_b)


def _rope_tables(s, c_len):
    tok = jnp.arange(s)
    row = (tok // GRID_W).astype(F32)
    col = (tok % GRID_W).astype(F32)
    half = HEAD // 2
    freqs = ROPE_THETA ** (-jnp.arange(0, half, 2, dtype=F32) / half)
    ang_r = row[:, None] * freqs
    ang_c = col[:, None] * freqs
    cos = jnp.concatenate([jnp.cos(ang_r)] * 2 + [jnp.cos(ang_c)] * 2, axis=1)
    sin = jnp.concatenate([-jnp.sin(ang_r), jnp.sin(ang_r), -jnp.sin(ang_c), jnp.sin(ang_c)], axis=1)
    cos = jnp.concatenate([jnp.ones((c_len, HEAD), F32), cos], axis=0)
    sin = jnp.concatenate([jnp.zeros((c_len, HEAD), F32), sin], axis=0)
    return jnp.tile(cos, (1, 2)), jnp.tile(sin, (1, 2))


def kernel(x, c, ctx, c_ctx, w_mod, b_mod, norm_mix, norm_mlp, mlp_w1, mlp_w2, rwkv_mu, rwkv_wr, rwkv_wk, rwkv_wv,
           rwkv_wo, rwkv_w0, rwkv_w1, rwkv_w2, rwkv_a0, rwkv_a1, rwkv_a2, rwkv_g1, rwkv_g2, rwkv_k_k, rwkv_k_a,
           rwkv_r_k, rwkv_ln_w, rwkv_ln_b, attn_wqkv, attn_q_norm, attn_k_norm, attn_wo, final_norm):
    b, s, d = x.shape
    c_len = ctx.shape[1]
    assert d % (4 * 128) == 0 and d // HEAD <= SEG_W
    assert s % TQ == 0 and TQ % TM == 0 and s % (2 * TM) == 0 and TM % GRID_W == 0 and TM % CHUNK == 0
    assert c_len == TM, "the context must fill exactly one token tile"
    assert w_mod.shape[0] == 2 and rwkv_mu.shape[0] == 1 and attn_wqkv.shape[0] == 1
    ctx_tiles = c_len // TM
    row = lambda a: a.reshape(1, -1).astype(F32)
    bf = lambda a: a.astype(BF16)
    consts = _constants(d)

    rows = -(-(b + 1) // 8) * 8
    cvec = jnp.zeros((rows, d), F32).at[:b].set(c).at[b].set(c_ctx)
    m = _modulation(cvec, w_mod, b_mod)
    m = m.reshape(2, rows, 6, d)
    mods = [jnp.stack([jnp.broadcast_to(m[l, b][None], (b, 6, d)), m[l, :b]], axis=1) for l in range(2)]

    p0 = dict(
        mu=rwkv_mu[0], wr=bf(rwkv_wr[0]), wk=bf(rwkv_wk[0]), wv=bf(rwkv_wv[0]), wo=bf(rwkv_wo[0]),
        g1=bf(rwkv_g1[0]), g2=bf(rwkv_g2[0]),
        w1=bf(jnp.concatenate([rwkv_w1[0, 0], rwkv_w1[0, 1]], axis=1)),
        w2=bf(_block_diag2(rwkv_w2[0, 0], rwkv_w2[0, 1])),
        w0=rwkv_w0[0].reshape(1, -1),
        a1=bf(jnp.concatenate([rwkv_a1[0, 0], rwkv_a1[0, 1]], axis=1)),
        a2=bf(_block_diag2(rwkv_a2[0, 0], rwkv_a2[0, 1])),
        a0=rwkv_a0[0].reshape(1, -1),
        k_k=row(rwkv_k_k[0]), k_a=row(rwkv_k_a[0]), r_k=row(rwkv_r_k[0]),
        ln_w=row(rwkv_ln_w[0]), ln_b=row(rwkv_ln_b[0]),
    )
    (v, g, bonus, a_f, r_f, b_f, k_f, a_b, r_b, b_b, k_b, dl_f, dl_b) = _proj0(
        ctx, x, mods[0], row(norm_mix[0]), p0, consts)
    w_f, u_f = _wkv_intra(False, a_f, b_f, k_f, v)
    w_b, u_b = _wkv_intra(True, a_b, b_b, k_b, v)
    y_f, y_b = _wkv_scan(ctx_tiles, (w_f, u_f, r_f, b_f, k_f, v, dl_f), (w_b, u_b, r_b, b_b, k_b, v, dl_b))
    readout = functools.partial(_readout0, yf=y_f, yb=y_b, bonus=bonus, g=g, mods=mods[0], p=p0,
                                gain_mlp=row(norm_mlp[0]), w1=bf(mlp_w1[0]), w2=bf(mlp_w2[0]), consts=consts)
    x1_ctx = readout(ctx, 0, 1, 0)
    x1_lat = readout(x, ctx_tiles, 2, 1)

    cos, sin = _rope_tables(s, c_len)
    qn = jnp.tile(attn_q_norm[0], d // HEAD).reshape(1, -1)
    kn = jnp.tile(attn_k_norm[0], KV_HEADS).reshape(1, -1)
    nqk = d + KV_HEADS * HEAD
    wqkv = attn_wqkv[0]
    q, k, vt = _qkv1(x1_ctx, x1_lat, mods[1], row(norm_mix[1]), bf(wqkv[:, :d]), bf(wqkv[:, d:nqk]),
                     bf(wqkv[:, nqk:].T), qn, kn, cos, sin, consts, ctx_tiles)
    att = _attention(q, k, vt)
    return _out1(x1_lat, att, mods[1], bf(attn_wo[0]), row(norm_mlp[1]), bf(mlp_w1[1]), bf(mlp_w2[1]),
                 row(final_norm))
```
